```python
import math
import jax, jax.numpy as jnp
from jax import lax
import numpy as np

D_MODEL = 2048
BATCH = 4
SEQ = 4096
DEPTH = 4

HEAD_DIM = 64
N_HEADS_A = D_MODEL // HEAD_DIM
N_KV_A = N_HEADS_A // 8
GROUP_A = N_HEADS_A // N_KV_A
N_HEADS_B = D_MODEL // HEAD_DIM
WINDOW = 128
BLOCK = 128
D_FF = 4 * D_MODEL
N_MIXERS = 2
N_A_LAYERS = (DEPTH + 1) // 2
N_B_LAYERS = DEPTH // 2
RMS_EPS = 1e-5

kernel_name = "hybrid_swa_sink_alibi_stickbreaking_sqrelu"


def rmsnorm(x, gain):
    xf = x.astype(jnp.float32)
    y = xf * lax.rsqrt(jnp.mean(xf * xf, axis=-1, keepdims=True) + RMS_EPS)
    return (y * gain.astype(jnp.float32)).astype(x.dtype)


def alibi_slopes(n_heads):
    return jnp.power(2.0, -8.0 * (jnp.arange(n_heads, dtype=jnp.float32) + 1.0) / n_heads)


def sliding_window_sink_attention(xn, w_qkv, w_o, sinks):
    b, s, _ = xn.shape
    nb = s // BLOCK
    qkv = xn @ w_qkv
    q_w, k_w = N_HEADS_A * HEAD_DIM, N_KV_A * HEAD_DIM
    q = qkv[..., :q_w].reshape(b, nb, BLOCK, N_KV_A, GROUP_A, HEAD_DIM)
    k = qkv[..., q_w:q_w + k_w].reshape(b, nb, BLOCK, N_KV_A, HEAD_DIM)
    v = qkv[..., q_w + k_w:].reshape(b, nb, BLOCK, N_KV_A, HEAD_DIM)
    pad = ((0, 0), (1, 0), (0, 0), (0, 0), (0, 0))
    k_win = jnp.concatenate([jnp.pad(k, pad)[:, :-1], k], axis=2)
    v_win = jnp.concatenate([jnp.pad(v, pad)[:, :-1], v], axis=2)
    scale = 1.0 / math.sqrt(HEAD_DIM)
    scores = jnp.einsum('bnqhgd,bnkhd->bnhgqk', q, k_win).astype(jnp.float32) * scale
    dist = (jnp.arange(BLOCK)[:, None] + BLOCK) - jnp.arange(2 * BLOCK)[None, :]
    key_abs = (jnp.arange(nb)[:, None] - 1) * BLOCK + jnp.arange(2 * BLOCK)[None, :]
    valid = ((dist >= 0) & (dist < WINDOW))[None, :, :] & (key_abs >= 0)[:, None, :]
    slopes = alibi_slopes(N_HEADS_A).reshape(N_KV_A, GROUP_A)
    scores = scores - slopes[None, None, :, :, None, None] * dist.astype(jnp.float32)
    scores = jnp.where(valid[None, :, None, None], scores, -jnp.inf)
    sink = sinks.astype(jnp.float32).reshape(N_KV_A, GROUP_A)[None, None, :, :, None, None]
    m = jnp.maximum(jnp.max(scores, axis=-1, keepdims=True), sink)
    p = jnp.exp(scores - m)
    p = p / (jnp.sum(p, axis=-1, keepdims=True) + jnp.exp(sink - m))
    out = jnp.einsum('bnhgqk,bnkhd->bnqhgd', p.astype(v.dtype), v_win)
    return out.reshape(b, s, N_HEADS_A * HEAD_DIM) @ w_o


def stick_breaking_attention(xn, w_qkv, w_o):
    b, s, _ = xn.shape
    nb = s // BLOCK
    qkv = xn @ w_qkv
    w = N_HEADS_B * HEAD_DIM
    q = qkv[..., :w].reshape(b, nb, BLOCK, N_HEADS_B, HEAD_DIM)
    k = qkv[..., w:2 * w].reshape(b, s, N_HEADS_B, HEAD_DIM)
    v = qkv[..., 2 * w:].reshape(b, s, N_HEADS_B, HEAD_DIM)
    q_blocks = jnp.moveaxis(q, 1, 0)
    starts = jnp.arange(nb, dtype=jnp.int32) * BLOCK
    scale = 1.0 / math.sqrt(HEAD_DIM)
    key_pos = jnp.arange(s, dtype=jnp.int32)

    def one_block(args):
        qb, start = args
        z = jnp.einsum('bqhd,bkhd->bhqk', qb, k).astype(jnp.float32) * scale
        t = start + jnp.arange(BLOCK, dtype=jnp.int32)
        before = (key_pos[None, :] < t[:, None])[None, None]
        log_beta = jax.nn.log_sigmoid(z)
        log_1m_beta = jnp.where(before, jax.nn.log_sigmoid(-z), 0.0)
        suffix = lax.cumsum(log_1m_beta, axis=3, reverse=True) - log_1m_beta
        a = jnp.where(before, jnp.exp(log_beta + suffix), 0.0)
        return jnp.einsum('bhqk,bkhd->bqhd', a.astype(v.dtype), v)

    out = lax.map(one_block, (q_blocks, starts))
    out = jnp.moveaxis(out, 0, 1).reshape(b, s, N_HEADS_B * HEAD_DIM)
    return out @ w_o


def squared_relu_mlp(xn, w_in, w_out):
    h = jax.nn.relu(xn @ w_in)
    return (h * h) @ w_out


def setup_inputs(seed: int = 0) -> dict:
    key = jax.random.key(seed)
    ks = jax.random.split(key, 12)
    d = D_MODEL
    qkv_a = (N_HEADS_A + 2 * N_KV_A) * HEAD_DIM
    qkv_b = 3 * N_HEADS_B * HEAD_DIM
    x = jax.random.normal(ks[0], (BATCH, SEQ, d), jnp.float32)
    a_w_qkv = jax.random.normal(ks[1], (N_A_LAYERS, d, qkv_a), jnp.float32) * d ** -0.5
    a_w_o = jax.random.normal(ks[2], (N_A_LAYERS, N_HEADS_A * HEAD_DIM, d), jnp.float32) * (N_HEADS_A * HEAD_DIM) ** -0.5
    a_sinks = jax.random.normal(ks[3], (N_A_LAYERS, N_HEADS_A), jnp.float32) * 0.5
    b_w_qkv = jax.random.normal(ks[4], (N_B_LAYERS, d, qkv_b), jnp.float32) * d ** -0.5
    b_w_o = jax.random.normal(ks[5], (N_B_LAYERS, N_HEADS_B * HEAD_DIM, d), jnp.float32) * (N_HEADS_B * HEAD_DIM) ** -0.5
    norm_mix = 1.0 + 0.02 * jax.random.normal(ks[6], (DEPTH, d), jnp.float32)
    norm_mlp = 1.0 + 0.02 * jax.random.normal(ks[7], (DEPTH, d), jnp.float32)
    mlp_w_in = jax.random.normal(ks[8], (DEPTH, d, D_FF), jnp.float32) * d ** -0.5
    mlp_w_out = jax.random.normal(ks[9], (DEPTH, D_FF, d), jnp.float32) * D_FF ** -0.5
    final_norm = 1.0 + 0.02 * jax.random.normal(ks[10], (d,), jnp.float32)
    return {"x": x, "a_w_qkv": a_w_qkv, "a_w_o": a_w_o, "a_sinks": a_sinks,
            "b_w_qkv": b_w_qkv, "b_w_o": b_w_o, "norm_mix": norm_mix,
            "norm_mlp": norm_mlp, "mlp_w_in": mlp_w_in, "mlp_w_out": mlp_w_out,
            "final_norm": final_norm}


def reference(x, a_w_qkv, a_w_o, a_sinks, b_w_qkv, b_w_o, norm_mix, norm_mlp,
              mlp_w_in, mlp_w_out, final_norm):
    for i in range(DEPTH):
        h = rmsnorm(x, norm_mix[i])
        j = i // N_MIXERS
        if i % N_MIXERS == 0:
            x = x + sliding_window_sink_attention(h, a_w_qkv[j], a_w_o[j], a_sinks[j])
        else:
            x = x + stick_breaking_attention(h, b_w_qkv[j], b_w_o[j])
        h = rmsnorm(x, norm_mlp[i])
        x = x + squared_relu_mlp(h, mlp_w_in[i], mlp_w_out[i])
    return rmsnorm(x, final_norm)
```

```python
import functools
import math

import jax
import jax.numpy as jnp
from jax import lax
from jax.experimental import pallas as pl
from jax.experimental.pallas import tpu as pltpu

HEAD_DIM = 64
GROUP_A = 8
WINDOW = 128
BLOCK = 128
RMS_EPS = 1e-5
N_MIXERS = 2
MASK_VALUE = -1e30
LOG_F32_UNDERFLOW = -104.0

VMEM_LIMIT_BYTES = 56 * 1024 * 1024
NORM_ROWS = 256

bf16 = jnp.bfloat16
f32 = jnp.float32


def _rmsnorm_rows(x, gain):
    y = x * lax.rsqrt(jnp.mean(x * x, axis=-1, keepdims=True) + RMS_EPS)
    return y * gain


def _store_normed(x_ref, gain_ref, xn_ref):
    rows = x_ref.shape[0]

    def step(c, _):
        r = pl.multiple_of(c * NORM_ROWS, NORM_ROWS)
        x = x_ref[pl.ds(r, NORM_ROWS), :]
        xn_ref[pl.ds(r, NORM_ROWS), :] = _rmsnorm_rows(x, gain_ref[...]).astype(xn_ref.dtype)
        return 0

    lax.fori_loop(0, rows // NORM_ROWS, step, 0)


def _norm_proj_kernel(x_ref, gain_ref, w_ref, o_ref, xn_ref):
    @pl.when(pl.program_id(1) == 0)
    def _():
        _store_normed(x_ref, gain_ref, xn_ref)

    o_ref[...] = jnp.dot(xn_ref[...], w_ref[...],
                         preferred_element_type=f32).astype(o_ref.dtype)


def _norm_proj(x, gain, w, *, tm, tn):
    m, d = x.shape
    n = w.shape[1]
    assert m % tm == 0 and n % tn == 0
    return pl.pallas_call(
        _norm_proj_kernel,
        out_shape=jax.ShapeDtypeStruct((m, n), bf16),
        grid=(m // tm, n // tn),
        in_specs=[
            pl.BlockSpec((tm, d), lambda i, j: (i, 0)),
            pl.BlockSpec((1, d), lambda i, j: (0, 0)),
            pl.BlockSpec((d, tn), lambda i, j: (0, j)),
        ],
        out_specs=pl.BlockSpec((tm, tn), lambda i, j: (i, j)),
        scratch_shapes=[pltpu.VMEM((tm, d), bf16)],
        compiler_params=pltpu.CompilerParams(
            dimension_semantics=("parallel", "arbitrary"),
            vmem_limit_bytes=VMEM_LIMIT_BYTES),
        name="norm_proj",
    )(x, gain, w)


def _proj_res_kernel(a_ref, w_ref, r_ref, o_ref):
    o_ref[...] = r_ref[...] + jnp.dot(a_ref[...], w_ref[...],
                                      preferred_element_type=f32)


def _proj_res(a, w, res, *, tm, tn):
    m, k = a.shape
    n = w.shape[1]
    assert m % tm == 0 and n % tn == 0
    return pl.pallas_call(
        _proj_res_kernel,
        out_shape=jax.ShapeDtypeStruct((m, n), f32),
        grid=(m // tm, n // tn),
        in_specs=[
            pl.BlockSpec((tm, k), lambda i, j: (i, 0)),
            pl.BlockSpec((k, tn), lambda i, j: (0, j)),
            pl.BlockSpec((tm, tn), lambda i, j: (i, j)),
        ],
        out_specs=pl.BlockSpec((tm, tn), lambda i, j: (i, j)),
        compiler_params=pltpu.CompilerParams(
            dimension_semantics=("parallel", "arbitrary"),
            vmem_limit_bytes=VMEM_LIMIT_BYTES),
        name="proj_res",
    )(a, w, res)


def _mlp_kernel(x_ref, gain_ref, win_ref, wout_ref, fgain_ref, o_ref, xn_ref, *,
                final_norm):
    f = pl.program_id(1)

    @pl.when(f == 0)
    def _():
        _store_normed(x_ref, gain_ref, xn_ref)
        o_ref[...] = x_ref[...]

    h = jnp.dot(xn_ref[...], win_ref[...], preferred_element_type=f32)
    h = jnp.maximum(h, 0.0)
    h = (h * h).astype(bf16)
    o_ref[...] += jnp.dot(h, wout_ref[...], preferred_element_type=f32)

    if final_norm:
        @pl.when(f == pl.num_programs(1) - 1)
        def _():
            _store_normed(o_ref, fgain_ref, o_ref)


def _mlp(x, gain, w_in, w_out, final_gain, *, tm, tf, final_norm):
    m, d = x.shape
    d_ff = w_in.shape[1]
    assert m % tm == 0 and d_ff % tf == 0
    return pl.pallas_call(
        functools.partial(_mlp_kernel, final_norm=final_norm),
        out_shape=jax.ShapeDtypeStruct((m, d), f32),
        grid=(m // tm, d_ff // tf),
        in_specs=[
            pl.BlockSpec((tm, d), lambda i, f: (i, 0), pipeline_mode=pl.Buffered(1)),
            pl.BlockSpec((1, d), lambda i, f: (0, 0)),
            pl.BlockSpec((d, tf), lambda i, f: (0, f)),
            pl.BlockSpec((tf, d), lambda i, f: (f, 0)),
            pl.BlockSpec((1, d), lambda i, f: (0, 0)),
        ],
        out_specs=pl.BlockSpec((tm, d), lambda i, f: (i, 0)),
        scratch_shapes=[pltpu.VMEM((tm, d), bf16)],
        compiler_params=pltpu.CompilerParams(
            dimension_semantics=("parallel", "arbitrary"),
            vmem_limit_bytes=VMEM_LIMIT_BYTES),
        name="mlp_final" if final_norm else "mlp",
    )(x, gain, w_in, w_out, final_gain)


def _swa_kernel(sink_ref, q_ref, kp_ref, kc_ref, vp_ref, vc_ref, o_ref, *,
                n_kv, slopes):
    nblk = pl.program_id(1)
    scale = 1.0 / math.sqrt(HEAD_DIM)
    qi = lax.broadcasted_iota(jnp.int32, (BLOCK, 2 * BLOCK), 0)
    kj = lax.broadcasted_iota(jnp.int32, (BLOCK, 2 * BLOCK), 1)
    dist = qi + BLOCK - kj
    valid = (dist >= 0) & (dist < WINDOW) & ((kj >= BLOCK) | (nblk > 0))
    distf = dist.astype(f32)
    nt_dims = (((1,), (1,)), ((), ()))

    for h in range(n_kv):
        ks = slice(h * HEAD_DIM, (h + 1) * HEAD_DIM)
        k_win = jnp.concatenate([kp_ref[0, :, ks], kc_ref[0, :, ks]], axis=0)
        v_win = jnp.concatenate([vp_ref[0, :, ks], vc_ref[0, :, ks]], axis=0)
        for g in range(GROUP_A):
            hd = h * GROUP_A + g
            qs = slice(hd * HEAD_DIM, (hd + 1) * HEAD_DIM)
            s = lax.dot_general(q_ref[0, :, qs], k_win, nt_dims,
                                preferred_element_type=f32) * scale
            s = s - slopes[hd] * distf
            s = jnp.where(valid, s, MASK_VALUE)
            sink = sink_ref[hd]
            m = jnp.maximum(jnp.max(s, axis=-1, keepdims=True), sink)
            p = jnp.exp(s - m)
            denom = jnp.sum(p, axis=-1, keepdims=True) + jnp.exp(sink - m)
            o = jnp.dot(p.astype(bf16), v_win, preferred_element_type=f32) / denom
            o_ref[0, :, qs] = o.astype(o_ref.dtype)


def _swa_attention(qkv, sinks, *, n_heads, n_kv):
    b, s, _ = qkv.shape
    q_w = n_heads * HEAD_DIM
    kv_w = n_kv * HEAD_DIM
    assert q_w % kv_w == 0
    k_col = q_w // kv_w
    v_col = k_col + 1
    slopes = tuple(2.0 ** (-8.0 * (h + 1.0) / n_heads) for h in range(n_heads))
    prev = lambda bi, n: jnp.maximum(n - 1, 0)
    return pl.pallas_call(
        functools.partial(_swa_kernel, n_kv=n_kv, slopes=slopes),
        out_shape=jax.ShapeDtypeStruct((b, s, q_w), bf16),
        grid=(b, s // BLOCK),
        in_specs=[
            pl.BlockSpec(memory_space=pltpu.SMEM),
            pl.BlockSpec((1, BLOCK, q_w), lambda bi, n: (bi, n, 0)),
            pl.BlockSpec((1, BLOCK, kv_w), lambda bi, n: (bi, prev(bi, n), k_col)),
            pl.BlockSpec((1, BLOCK, kv_w), lambda bi, n: (bi, n, k_col)),
            pl.BlockSpec((1, BLOCK, kv_w), lambda bi, n: (bi, prev(bi, n), v_col)),
            pl.BlockSpec((1, BLOCK, kv_w), lambda bi, n: (bi, n, v_col)),
        ],
        out_specs=pl.BlockSpec((1, BLOCK, q_w), lambda bi, n: (bi, n, 0)),
        compiler_params=pltpu.CompilerParams(
            dimension_semantics=("parallel", "arbitrary"),
            vmem_limit_bytes=VMEM_LIMIT_BYTES),
        name="swa_attention",
    )(sinks, qkv, qkv, qkv, qkv, qkv)


LANES = 128
HEADS_PER_STEP = LANES // HEAD_DIM


def _sb_kernel(q_ref, k_ref, v_ref, o_ref):
    qblk = pl.program_id(2)
    scale = 1.0 / math.sqrt(HEAD_DIM)
    row = lax.broadcasted_iota(jnp.int32, (BLOCK, BLOCK), 0)
    col = lax.broadcasted_iota(jnp.int32, (BLOCK, BLOCK), 1)
    before = col < row
    jj = lax.broadcasted_iota(jnp.int32, (BLOCK, 2 * BLOCK), 0)
    ss = lax.broadcasted_iota(jnp.int32, (BLOCK, 2 * BLOCK), 1)
    tri = jnp.where((ss >= BLOCK) | (jj > ss), 1.0, 0.0).astype(bf16)
    lane = lax.broadcasted_iota(jnp.int32, (BLOCK, 2 * HEAD_DIM), 1)
    nt_dims = (((1,), (1,)), ((), ()))

    def sweep_head(hh):
        in_head = (lane >= hh * HEAD_DIM) & (lane < (hh + 1) * HEAD_DIM)
        q = jnp.where(in_head, q_ref[0], jnp.zeros_like(q_ref[0]))

        def tile(j, carry, acc, diag):
            r = pl.multiple_of(j * BLOCK, BLOCK)
            k = k_ref[0, pl.ds(r, BLOCK), :]
            v = v_ref[0, pl.ds(r, BLOCK), :]
            z = lax.dot_general(q, k, nt_dims, preferred_element_type=f32) * scale
            soft = jnp.log1p(jnp.exp(-jnp.abs(z)))
            log_beta = jnp.minimum(z, 0.0) - soft
            log_1m = log_beta - z
            if diag:
                log_1m = jnp.where(before, log_1m, 0.0)
            hi = log_1m.astype(bf16)
            lo = (log_1m - hi.astype(f32)).astype(bf16)
            sums = (jnp.dot(hi, tri, preferred_element_type=f32)
                    + jnp.dot(lo, tri, preferred_element_type=f32))
            a = jnp.exp(log_beta + sums[:, :BLOCK] + carry)
            if diag:
                a = jnp.where(before, a, 0.0)
            acc = acc + jnp.dot(a.astype(bf16), v, preferred_element_type=f32)
            carry = carry + sums[:, BLOCK:]
            return carry, acc

        carry, acc = tile(qblk, jnp.zeros((BLOCK, BLOCK), f32),
                          jnp.zeros((BLOCK, 2 * HEAD_DIM), f32), True)

        def cond(st):
            j, cmax, _, _ = st
            return (j >= 0) & (cmax > LOG_F32_UNDERFLOW)

        def body(st):
            j, _, carry, acc = st
            carry, acc = tile(j, carry, acc, False)
            return j - 1, jnp.max(carry), carry, acc

        _, _, _, acc = lax.while_loop(cond, body, (qblk - 1, jnp.max(carry), carry, acc))
        return acc, in_head

    out = jnp.zeros((BLOCK, 2 * HEAD_DIM), f32)
    for hh in range(HEADS_PER_STEP):
        acc, in_head = sweep_head(hh)
        out = jnp.where(in_head, acc, out)
    o_ref[0] = out.astype(o_ref.dtype)


def _sb_attention(qkv, *, n_heads):
    b, s, _ = qkv.shape
    w = n_heads * HEAD_DIM
    lanes = HEADS_PER_STEP * HEAD_DIM
    n_col = w // lanes
    return pl.pallas_call(
        _sb_kernel,
        out_shape=jax.ShapeDtypeStruct((b, s, w), bf16),
        grid=(b, n_col, s // BLOCK),
        in_specs=[
            pl.BlockSpec((1, BLOCK, lanes), lambda bi, c, n: (bi, n, c)),
            pl.BlockSpec((1, s, lanes), lambda bi, c, n: (bi, 0, n_col + c)),
            pl.BlockSpec((1, s, lanes), lambda bi, c, n: (bi, 0, 2 * n_col + c)),
        ],
        out_specs=pl.BlockSpec((1, BLOCK, lanes), lambda bi, c, n: (bi, n, c)),
        compiler_params=pltpu.CompilerParams(
            dimension_semantics=("parallel", "parallel", "arbitrary"),
            vmem_limit_bytes=VMEM_LIMIT_BYTES),
        name="sb_attention",
    )(qkv, qkv, qkv)


def kernel(x, a_w_qkv, a_w_o, a_sinks, b_w_qkv, b_w_o, norm_mix, norm_mlp,
           mlp_w_in, mlp_w_out, final_norm):
    b, s, d = x.shape
    depth = norm_mix.shape[0]
    n_heads_a = a_w_o.shape[1] // HEAD_DIM
    n_kv_a = (a_w_qkv.shape[2] // HEAD_DIM - n_heads_a) // 2
    n_heads_b = b_w_o.shape[1] // HEAD_DIM
    assert n_heads_a == n_kv_a * GROUP_A and s % BLOCK == 0

    xf = x.reshape(b * s, d)
    final_gain = final_norm.reshape(1, d)
    for i in range(depth):
        j = i // N_MIXERS
        gain_mix = norm_mix[i].reshape(1, d)
        if i % N_MIXERS == 0:
            qkv = _norm_proj(xf, gain_mix, a_w_qkv[j].astype(bf16), tm=1024, tn=512)
            att = _swa_attention(qkv.reshape(b, s, -1), a_sinks[j],
                                 n_heads=n_heads_a, n_kv=n_kv_a)
            w_o = a_w_o[j]
        else:
            qkv = _norm_proj(xf, gain_mix, b_w_qkv[j].astype(bf16), tm=1024, tn=1024)
            att = _sb_attention(qkv.reshape(b, s, -1), n_heads=n_heads_b)
            w_o = b_w_o[j]
        xf = _proj_res(att.reshape(b * s, -1), w_o.astype(bf16), xf, tm=1024, tn=1024)
        xf = _mlp(xf, norm_mlp[i].reshape(1, d), mlp_w_in[i].astype(bf16),
                  mlp_w_out[i].astype(bf16), final_gain, tm=1024, tf=512,
                  final_norm=(i == depth - 1))
    return xf.reshape(b, s, d)
```

```python
import functools
import math

import jax
import jax.numpy as jnp
from jax import lax
from jax.experimental import pallas as pl
from jax.experimental.pallas import tpu as pltpu

HEAD_DIM = 64
GROUP_A = 8
WINDOW = 128
BLOCK = 128
RMS_EPS = 1e-5
N_MIXERS = 2
LANES = 128
HEADS_PER_LANE_BLOCK = LANES // HEAD_DIM
MASK_VALUE = -1e30
LOG2E = math.log2(math.e)
LOG2_F32_UNDERFLOW = -150.0
SCALE = 1.0 / math.sqrt(HEAD_DIM)
assert math.frexp(SCALE)[0] == 0.5, "scale must be a power of two to fold into bf16 queries exactly"

VMEM_LIMIT_BYTES = 56 * 1024 * 1024
NORM_ROWS = 256
SB_HEADS = 8

bf16 = jnp.bfloat16
f32 = jnp.float32
NT_DIMS = (((1,), (1,)), ((), ()))


def _rmsnorm_rows(x, gain):
    y = x * lax.rsqrt(jnp.mean(x * x, axis=-1, keepdims=True) + RMS_EPS)
    return y * gain


def _store_normed(x_ref, gain_ref, xn_ref):
    rows = x_ref.shape[0]

    def step(c, _):
        r = pl.multiple_of(c * NORM_ROWS, NORM_ROWS)
        x = x_ref[pl.ds(r, NORM_ROWS), :]
        xn_ref[pl.ds(r, NORM_ROWS), :] = _rmsnorm_rows(x, gain_ref[...]).astype(xn_ref.dtype)
        return 0

    lax.fori_loop(0, rows // NORM_ROWS, step, 0)


def _norm_proj_kernel(x_ref, gain_ref, w_ref, o_ref, xn_ref):
    @pl.when(pl.program_id(1) == 0)
    def _():
        _store_normed(x_ref, gain_ref, xn_ref)

    o_ref[...] = jnp.dot(xn_ref[...], w_ref[...],
                         preferred_element_type=f32).astype(o_ref.dtype)


def _norm_proj(x, gain, w, layer, *, tm, tn):
    m, d = x.shape
    n = w.shape[2]
    assert m % tm == 0 and n % tn == 0
    return pl.pallas_call(
        _norm_proj_kernel,
        out_shape=jax.ShapeDtypeStruct((m, n), bf16),
        grid=(m // tm, n // tn),
        in_specs=[
            pl.BlockSpec((tm, d), lambda i, j: (i, 0)),
            pl.BlockSpec((1, d), lambda i, j: (0, 0)),
            pl.BlockSpec((None, d, tn), lambda i, j: (layer, 0, j)),
        ],
        out_specs=pl.BlockSpec((tm, tn), lambda i, j: (i, j)),
        scratch_shapes=[pltpu.VMEM((tm, d), bf16)],
        compiler_params=pltpu.CompilerParams(
            dimension_semantics=("parallel", "arbitrary"),
            vmem_limit_bytes=VMEM_LIMIT_BYTES),
        name="norm_proj",
    )(x, gain, w)


def _proj_res_kernel(a_ref, w_ref, r_ref, o_ref):
    o_ref[...] = r_ref[...] + jnp.dot(a_ref[...], w_ref[...],
                                      preferred_element_type=f32)


def _proj_res(a, w, layer, res, *, tm):
    m, k = a.shape
    n = w.shape[2]
    assert m % tm == 0
    return pl.pallas_call(
        _proj_res_kernel,
        out_shape=jax.ShapeDtypeStruct((m, n), f32),
        grid=(m // tm,),
        in_specs=[
            pl.BlockSpec((tm, k), lambda i: (i, 0)),
            pl.BlockSpec((None, k, n), lambda i: (layer, 0, 0)),
            pl.BlockSpec((tm, n), lambda i: (i, 0)),
        ],
        out_specs=pl.BlockSpec((tm, n), lambda i: (i, 0)),
        compiler_params=pltpu.CompilerParams(
            dimension_semantics=("parallel",),
            vmem_limit_bytes=VMEM_LIMIT_BYTES),
        name="proj_res",
    )(a, w, res)


def _mlp_kernel(x_ref, gain_ref, win_ref, wout_ref, fgain_ref, o_ref, xn_ref, *,
                final_norm):
    f = pl.program_id(1)

    @pl.when(f == 0)
    def _():
        _store_normed(x_ref, gain_ref, xn_ref)
        o_ref[...] = x_ref[...]

    h = jnp.dot(xn_ref[...], win_ref[...], preferred_element_type=f32)
    h = jnp.maximum(h, 0.0)
    h = (h * h).astype(bf16)
    o_ref[...] += jnp.dot(h, wout_ref[...], preferred_element_type=f32)

    if final_norm:
        @pl.when(f == pl.num_programs(1) - 1)
        def _():
            _store_normed(o_ref, fgain_ref, o_ref)


def _mlp(x, gain, w_in, w_out, layer, final_gain, *, tm, tf, final_norm):
    m, d = x.shape
    d_ff = w_in.shape[2]
    assert m % tm == 0 and d_ff % tf == 0
    return pl.pallas_call(
        functools.partial(_mlp_kernel, final_norm=final_norm),
        out_shape=jax.ShapeDtypeStruct((m, d), f32),
        grid=(m // tm, d_ff // tf),
        in_specs=[
            pl.BlockSpec((tm, d), lambda i, f: (i, 0), pipeline_mode=pl.Buffered(1)),
            pl.BlockSpec((1, d), lambda i, f: (0, 0)),
            pl.BlockSpec((None, d, tf), lambda i, f: (layer, 0, f)),
            pl.BlockSpec((None, tf, d), lambda i, f: (layer, f, 0)),
            pl.BlockSpec((1, d), lambda i, f: (0, 0)),
        ],
        out_specs=pl.BlockSpec((tm, d), lambda i, f: (i, 0)),
        scratch_shapes=[pltpu.VMEM((tm, d), bf16)],
        compiler_params=pltpu.CompilerParams(
            dimension_semantics=("parallel", "arbitrary"),
            vmem_limit_bytes=VMEM_LIMIT_BYTES),
        name="mlp_final" if final_norm else "mlp",
    )(x, gain, w_in, w_out, final_gain)


def _split_head_pair(qp):
    low = lax.broadcasted_iota(jnp.int32, qp.shape, 1) < HEAD_DIM
    zero = jnp.zeros_like(qp)
    return jnp.concatenate([jnp.where(low, qp, zero), jnp.where(low, zero, qp)], axis=0)


def _swa_kernel(sink_ref, q_ref, kp_ref, kc_ref, vp_ref, vc_ref, o_ref, *,
                n_kv, slopes):
    nblk = pl.program_id(1)
    qi = lax.broadcasted_iota(jnp.int32, (BLOCK, 2 * BLOCK), 0)
    kj = lax.broadcasted_iota(jnp.int32, (BLOCK, 2 * BLOCK), 1)
    dist = qi + BLOCK - kj
    valid = (dist >= 0) & (dist < WINDOW) & ((kj >= BLOCK) | (nblk > 0))
    distf = dist.astype(f32)
    low = lax.broadcasted_iota(jnp.int32, (BLOCK, LANES), 1) < HEAD_DIM
    low_kv = lax.broadcasted_iota(jnp.int32, (2 * BLOCK, LANES), 1) < HEAD_DIM
    pairs = GROUP_A // HEADS_PER_LANE_BLOCK

    def band(prev_ref, cur_ref, h):
        c = h // HEADS_PER_LANE_BLOCK
        cols = slice(c * LANES, (c + 1) * LANES)
        x = jnp.concatenate([prev_ref[0, :, cols], cur_ref[0, :, cols]], axis=0)
        swapped = pltpu.roll(x, HEAD_DIM, axis=1)
        if h % HEADS_PER_LANE_BLOCK == 0:
            return jnp.where(low_kv, x, swapped)
        return jnp.where(low_kv, swapped, x)

    def scores(h):
        k2 = band(kp_ref, kc_ref, h)
        qs = []
        for p in range(pairs):
            c = h * pairs + p
            qs.append(_split_head_pair(q_ref[0, :, c * LANES:(c + 1) * LANES] * SCALE))
        return lax.dot_general(jnp.concatenate(qs, axis=0), k2, NT_DIMS,
                               preferred_element_type=f32)

    def softmax_pv(h, s_all):
        v2 = band(vp_ref, vc_ref, h)
        ps, denoms = [], []
        for g in range(GROUP_A):
            hd = h * GROUP_A + g
            s = s_all[g * BLOCK:(g + 1) * BLOCK] - slopes[hd] * distf
            s = jnp.where(valid, s, MASK_VALUE)
            sink = sink_ref[hd]
            m = jnp.maximum(jnp.max(s, axis=-1, keepdims=True), sink)
            p = jnp.exp(s - m)
            denoms.append(jnp.sum(p, axis=-1, keepdims=True) + jnp.exp(sink - m))
            ps.append(p.astype(bf16))
        o_all = jnp.dot(jnp.concatenate(ps, axis=0), v2, preferred_element_type=f32)
        for p in range(pairs):
            g = HEADS_PER_LANE_BLOCK * p
            o_lo = o_all[g * BLOCK:(g + 1) * BLOCK] / denoms[g]
            o_hi = o_all[(g + 1) * BLOCK:(g + 2) * BLOCK] / denoms[g + 1]
            c = h * pairs + p
            o_ref[0, :, c * LANES:(c + 1) * LANES] = jnp.where(low, o_lo, o_hi).astype(o_ref.dtype)

    s_next = scores(0)
    for h in range(n_kv):
        s_cur = s_next
        if h + 1 < n_kv:
            s_next = scores(h + 1)
        softmax_pv(h, s_cur)


def _swa_attention(qkv, sinks, *, n_heads, n_kv):
    b, s, _ = qkv.shape
    q_w = n_heads * HEAD_DIM
    kv_w = n_kv * HEAD_DIM
    assert q_w % kv_w == 0 and kv_w % LANES == 0
    k_col = q_w // kv_w
    v_col = k_col + 1
    slopes = tuple(2.0 ** (-8.0 * (h + 1.0) / n_heads) for h in range(n_heads))
    prev = lambda n: jnp.maximum(n - 1, 0)
    return pl.pallas_call(
        functools.partial(_swa_kernel, n_kv=n_kv, slopes=slopes),
        out_shape=jax.ShapeDtypeStruct((b, s, q_w), bf16),
        grid=(b, s // BLOCK),
        in_specs=[
            pl.BlockSpec(memory_space=pltpu.SMEM),
            pl.BlockSpec((1, BLOCK, q_w), lambda bi, n: (bi, n, 0)),
            pl.BlockSpec((1, BLOCK, kv_w), lambda bi, n: (bi, prev(n), k_col)),
            pl.BlockSpec((1, BLOCK, kv_w), lambda bi, n: (bi, n, k_col)),
            pl.BlockSpec((1, BLOCK, kv_w), lambda bi, n: (bi, prev(n), v_col)),
            pl.BlockSpec((1, BLOCK, kv_w), lambda bi, n: (bi, n, v_col)),
        ],
        out_specs=pl.BlockSpec((1, BLOCK, q_w), lambda bi, n: (bi, n, 0)),
        compiler_params=pltpu.CompilerParams(
            dimension_semantics=("parallel", "arbitrary"),
            vmem_limit_bytes=VMEM_LIMIT_BYTES),
        name="swa_attention",
    )(sinks, qkv, qkv, qkv, qkv, qkv)


def _sb_kernel(q_ref, k_ref, v_ref, o_ref, qm_ref, carry_ref, acc_ref):
    qblk = pl.program_id(2)
    n_pairs = qm_ref.shape[0]
    rows = HEADS_PER_LANE_BLOCK * BLOCK
    row = lax.broadcasted_iota(jnp.int32, (rows, BLOCK), 0) % BLOCK
    col = lax.broadcasted_iota(jnp.int32, (rows, BLOCK), 1)
    before = col < row
    jj = lax.broadcasted_iota(jnp.int32, (2 * BLOCK, 2 * BLOCK), 0) % BLOCK
    ss = lax.broadcasted_iota(jnp.int32, (2 * BLOCK, 2 * BLOCK), 1)
    tri = jnp.where((ss >= BLOCK) | (jj > ss), 1.0, 0.0).astype(bf16)
    sign_bit = jnp.uint32(0x80000000)

    for c in range(n_pairs):
        qm_ref[c] = _split_head_pair(q_ref[0, :, c * LANES:(c + 1) * LANES])

    def sweep(j, diag):
        r = pl.multiple_of(j * BLOCK, BLOCK)
        ys = []
        for c in range(n_pairs):
            k = k_ref[0, pl.ds(r, BLOCK), c * LANES:(c + 1) * LANES]
            z = lax.dot_general(qm_ref[c], k, NT_DIMS, preferred_element_type=f32)
            ys.append(z * (SCALE * LOG2E))
        log_betas, parts = [], []
        for y in ys:
            neg_abs = lax.bitcast_convert_type(
                lax.bitcast_convert_type(y, jnp.uint32) | sign_bit, f32)
            soft = jnp.log2(1.0 + jnp.exp2(neg_abs))
            log_beta = jnp.minimum(y, 0.0) - soft
            log_1m = log_beta - y
            if diag:
                log_1m = jnp.where(before, log_1m, 0.0)
            hi = log_1m.astype(bf16)
            lo = (log_1m - hi.astype(f32)).astype(bf16)
            log_betas.append(log_beta)
            parts.append(jnp.concatenate([hi, lo], axis=1))
        sums = jnp.dot(jnp.concatenate(parts, axis=0), tri, preferred_element_type=f32)
        cmax = None
        for c in range(n_pairs):
            s_c = sums[c * rows:(c + 1) * rows]
            x = log_betas[c] + s_c[:, :BLOCK]
            total = s_c[:, BLOCK:]
            if diag:
                a = jnp.where(before, jnp.exp2(x), 0.0)
                carry = total
            else:
                a = jnp.exp2(x + carry_ref[c])
                carry = carry_ref[c] + total
            v = v_ref[0, pl.ds(r, BLOCK), c * LANES:(c + 1) * LANES]
            pv = jnp.dot(a.astype(bf16), v, preferred_element_type=f32)
            if diag:
                acc_ref[c] = pv
            else:
                acc_ref[c] += pv
            carry_ref[c] = carry
            cmax = carry if cmax is None else jnp.maximum(cmax, carry)
        return jnp.max(cmax)

    def cond(st):
        j, cmax = st
        return (j >= 0) & (cmax > LOG2_F32_UNDERFLOW)

    def body(st):
        j, _ = st
        return j - 1, sweep(j, False)

    lax.while_loop(cond, body, (qblk - 1, sweep(qblk, True)))

    low = lax.broadcasted_iota(jnp.int32, (BLOCK, LANES), 1) < HEAD_DIM
    for c in range(n_pairs):
        o_ref[0, :, c * LANES:(c + 1) * LANES] = jnp.where(
            low, acc_ref[c, :BLOCK], acc_ref[c, BLOCK:]).astype(o_ref.dtype)


def _sb_attention(qkv, *, n_heads):
    b, s, _ = qkv.shape
    w = n_heads * HEAD_DIM
    n_pairs = SB_HEADS // HEADS_PER_LANE_BLOCK
    width = SB_HEADS * HEAD_DIM
    assert w % width == 0
    n_col = w // width
    rows = HEADS_PER_LANE_BLOCK * BLOCK
    return pl.pallas_call(
        _sb_kernel,
        out_shape=jax.ShapeDtypeStruct((b, s, w), bf16),
        grid=(b, n_col, s // BLOCK),
        in_specs=[
            pl.BlockSpec((1, BLOCK, width), lambda bi, c, n: (bi, n, c)),
            pl.BlockSpec((1, s, width), lambda bi, c, n: (bi, 0, n_col + c)),
            pl.BlockSpec((1, s, width), lambda bi, c, n: (bi, 0, 2 * n_col + c)),
        ],
        out_specs=pl.BlockSpec((1, BLOCK, width), lambda bi, c, n: (bi, n, c)),
        scratch_shapes=[
            pltpu.VMEM((n_pairs, rows, LANES), bf16),
            pltpu.VMEM((n_pairs, rows, LANES), f32),
            pltpu.VMEM((n_pairs, rows, LANES), f32),
        ],
        compiler_params=pltpu.CompilerParams(
            dimension_semantics=("parallel", "parallel", "arbitrary"),
            vmem_limit_bytes=VMEM_LIMIT_BYTES),
        name="sb_attention",
    )(qkv, qkv, qkv)


def kernel(x, a_w_qkv, a_w_o, a_sinks, b_w_qkv, b_w_o, norm_mix, norm_mlp,
           mlp_w_in, mlp_w_out, final_norm):
    b, s, d = x.shape
    depth = norm_mix.shape[0]
    n_heads_a = a_w_o.shape[1] // HEAD_DIM
    n_kv_a = (a_w_qkv.shape[2] // HEAD_DIM - n_heads_a) // 2
    n_heads_b = b_w_o.shape[1] // HEAD_DIM
    assert n_heads_a == n_kv_a * GROUP_A and s % BLOCK == 0

    a_w_qkv, a_w_o, b_w_qkv, b_w_o, mlp_w_in, mlp_w_out = (
        w.astype(bf16) for w in (a_w_qkv, a_w_o, b_w_qkv, b_w_o, mlp_w_in, mlp_w_out))
    xf = x.reshape(b * s, d)
    final_gain = final_norm.reshape(1, d)
    for i in range(depth):
        j = i // N_MIXERS
        gain_mix = norm_mix[i].reshape(1, d)
        if i % N_MIXERS == 0:
            qkv = _norm_proj(xf, gain_mix, a_w_qkv, j, tm=1024, tn=1280)
            att = _swa_attention(qkv.reshape(b, s, -1), a_sinks[j],
                                 n_heads=n_heads_a, n_kv=n_kv_a)
            w_o = a_w_o
        else:
            qkv = _norm_proj(xf, gain_mix, b_w_qkv, j, tm=1024, tn=1024)
            att = _sb_attention(qkv.reshape(b, s, -1), n_heads=n_heads_b)
            w_o = b_w_o
        xf = _proj_res(att.reshape(b * s, -1), w_o, j, xf, tm=512)
        xf = _mlp(xf, norm_mlp[i].reshape(1, d), mlp_w_in, mlp_w_out, i, final_gain,
                  tm=1024, tf=1024, final_norm=(i == depth - 1))
    return xf.reshape(b, s, d)
```

```python
import functools
import math

import jax
import jax.numpy as jnp
from jax import lax
from jax.experimental import pallas as pl
from jax.experimental.pallas import tpu as pltpu

HEAD_DIM = 64
GROUP_A = 8
WINDOW = 128
BLOCK = 128
RMS_EPS = 1e-5
N_MIXERS = 2
LANES = 128
HEADS_PER_LANE_BLOCK = LANES // HEAD_DIM
MASK_VALUE = -1e30
LOG2E = math.log2(math.e)
LOG2_F32_UNDERFLOW = -150.0
SCALE = 1.0 / math.sqrt(HEAD_DIM)
assert math.frexp(SCALE)[0] == 0.5, "scale must be a power of two to fold into bf16 queries exactly"

VMEM_LIMIT_BYTES = 56 * 1024 * 1024
NORM_ROWS = 256
SB_HEADS = 16

bf16 = jnp.bfloat16
f32 = jnp.float32
NT_DIMS = (((1,), (1,)), ((), ()))


def _rmsnorm_rows(x, gain):
    y = x * lax.rsqrt(jnp.mean(x * x, axis=-1, keepdims=True) + RMS_EPS)
    return y * gain


def _store_normed(x_ref, gain_ref, xn_ref):
    rows = x_ref.shape[0]

    def step(c, _):
        r = pl.multiple_of(c * NORM_ROWS, NORM_ROWS)
        x = x_ref[pl.ds(r, NORM_ROWS), :]
        xn_ref[pl.ds(r, NORM_ROWS), :] = _rmsnorm_rows(x, gain_ref[...]).astype(xn_ref.dtype)
        return 0

    lax.fori_loop(0, rows // NORM_ROWS, step, 0)


def _norm_proj_kernel(x_ref, gain_ref, w_ref, o_ref, xn_ref):
    @pl.when(pl.program_id(1) == 0)
    def _():
        _store_normed(x_ref, gain_ref, xn_ref)

    o_ref[...] = jnp.dot(xn_ref[...], w_ref[...],
                         preferred_element_type=f32).astype(o_ref.dtype)


def _norm_proj(x, gain, w, layer, *, tm, tn):
    m, d = x.shape
    n = w.shape[2]
    assert m % tm == 0 and n % tn == 0
    return pl.pallas_call(
        _norm_proj_kernel,
        out_shape=jax.ShapeDtypeStruct((m, n), bf16),
        grid=(m // tm, n // tn),
        in_specs=[
            pl.BlockSpec((tm, d), lambda i, j: (i, 0)),
            pl.BlockSpec((1, d), lambda i, j: (0, 0)),
            pl.BlockSpec((None, d, tn), lambda i, j: (layer, 0, j)),
        ],
        out_specs=pl.BlockSpec((tm, tn), lambda i, j: (i, j)),
        scratch_shapes=[pltpu.VMEM((tm, d), bf16)],
        compiler_params=pltpu.CompilerParams(
            dimension_semantics=("parallel", "arbitrary"),
            vmem_limit_bytes=VMEM_LIMIT_BYTES),
        name="norm_proj",
    )(x, gain, w)


def _proj_res_kernel(a_ref, w_ref, r_ref, o_ref):
    o_ref[...] = r_ref[...] + jnp.dot(a_ref[...], w_ref[...],
                                      preferred_element_type=f32)


def _proj_res(a, w, layer, res, *, tm):
    m, k = a.shape
    n = w.shape[2]
    assert m % tm == 0
    return pl.pallas_call(
        _proj_res_kernel,
        out_shape=jax.ShapeDtypeStruct((m, n), f32),
        grid=(m // tm,),
        in_specs=[
            pl.BlockSpec((tm, k), lambda i: (i, 0)),
            pl.BlockSpec((None, k, n), lambda i: (layer, 0, 0)),
            pl.BlockSpec((tm, n), lambda i: (i, 0)),
        ],
        out_specs=pl.BlockSpec((tm, n), lambda i: (i, 0)),
        compiler_params=pltpu.CompilerParams(
            dimension_semantics=("parallel",),
            vmem_limit_bytes=VMEM_LIMIT_BYTES),
        name="proj_res",
    )(a, w, res)


def _mlp_kernel(x_ref, gain_ref, win_ref, wout_ref, fgain_ref, o_ref, xn_ref, *,
                final_norm):
    f = pl.program_id(1)

    @pl.when(f == 0)
    def _():
        _store_normed(x_ref, gain_ref, xn_ref)
        o_ref[...] = x_ref[...]

    h = jnp.dot(xn_ref[...], win_ref[...], preferred_element_type=f32)
    h = jnp.maximum(h, 0.0)
    h = (h * h).astype(bf16)
    o_ref[...] += jnp.dot(h, wout_ref[...], preferred_element_type=f32)

    if final_norm:
        @pl.when(f == pl.num_programs(1) - 1)
        def _():
            _store_normed(o_ref, fgain_ref, o_ref)


def _mlp(x, gain, w_in, w_out, layer, final_gain, *, tm, tf, final_norm):
    m, d = x.shape
    d_ff = w_in.shape[2]
    assert m % tm == 0 and d_ff % tf == 0
    return pl.pallas_call(
        functools.partial(_mlp_kernel, final_norm=final_norm),
        out_shape=jax.ShapeDtypeStruct((m, d), f32),
        grid=(m // tm, d_ff // tf),
        in_specs=[
            pl.BlockSpec((tm, d), lambda i, f: (i, 0), pipeline_mode=pl.Buffered(1)),
            pl.BlockSpec((1, d), lambda i, f: (0, 0)),
            pl.BlockSpec((None, d, tf), lambda i, f: (layer, 0, f)),
            pl.BlockSpec((None, tf, d), lambda i, f: (layer, f, 0)),
            pl.BlockSpec((1, d), lambda i, f: (0, 0)),
        ],
        out_specs=pl.BlockSpec((tm, d), lambda i, f: (i, 0)),
        scratch_shapes=[pltpu.VMEM((tm, d), bf16)],
        compiler_params=pltpu.CompilerParams(
            dimension_semantics=("parallel", "arbitrary"),
            vmem_limit_bytes=VMEM_LIMIT_BYTES),
        name="mlp_final" if final_norm else "mlp",
    )(x, gain, w_in, w_out, final_gain)


def _split_head_pair(qp):
    low = lax.broadcasted_iota(jnp.int32, qp.shape, 1) < HEAD_DIM
    zero = jnp.zeros_like(qp)
    return jnp.concatenate([jnp.where(low, qp, zero), jnp.where(low, zero, qp)], axis=0)


def _swa_kernel(sink_ref, q_ref, kp_ref, kc_ref, vp_ref, vc_ref, o_ref, bias_ref, *,
                n_kv, slopes):
    nblk = pl.program_id(1)
    n_heads = n_kv * GROUP_A

    @pl.when((pl.program_id(0) == 0) & (nblk == 0))
    def _():
        qi = lax.broadcasted_iota(jnp.int32, (BLOCK, 2 * BLOCK), 0)
        kj = lax.broadcasted_iota(jnp.int32, (BLOCK, 2 * BLOCK), 1)
        dist = qi + BLOCK - kj
        in_band = (dist >= 0) & (dist < WINDOW)
        distf = dist.astype(f32)
        for hd in range(n_heads):
            bias = jnp.where(in_band, (-slopes[hd] * LOG2E) * distf, MASK_VALUE)
            bias_ref[0, hd] = bias
            bias_ref[1, hd] = jnp.where(kj >= BLOCK, bias, MASK_VALUE)

    first = (nblk == 0).astype(jnp.int32)
    low = lax.broadcasted_iota(jnp.int32, (BLOCK, LANES), 1) < HEAD_DIM
    low_kv = lax.broadcasted_iota(jnp.int32, (2 * BLOCK, LANES), 1) < HEAD_DIM
    pairs = GROUP_A // HEADS_PER_LANE_BLOCK

    def band(prev_ref, cur_ref, h):
        c = h // HEADS_PER_LANE_BLOCK
        cols = slice(c * LANES, (c + 1) * LANES)
        x = jnp.concatenate([prev_ref[0, :, cols], cur_ref[0, :, cols]], axis=0)
        swapped = pltpu.roll(x, HEAD_DIM, axis=1)
        if h % HEADS_PER_LANE_BLOCK == 0:
            return jnp.where(low_kv, x, swapped)
        return jnp.where(low_kv, swapped, x)

    def scores(h):
        k2 = band(kp_ref, kc_ref, h)
        qs = []
        for p in range(pairs):
            c = h * pairs + p
            qs.append(_split_head_pair(q_ref[0, :, c * LANES:(c + 1) * LANES] * SCALE))
        return lax.dot_general(jnp.concatenate(qs, axis=0), k2, NT_DIMS,
                               preferred_element_type=f32)

    def softmax_pv(h, s_all):
        v2 = band(vp_ref, vc_ref, h)
        ps, denoms = [], []
        for g in range(GROUP_A):
            hd = h * GROUP_A + g
            s = s_all[g * BLOCK:(g + 1) * BLOCK] * LOG2E + bias_ref[first, hd]
            sink = sink_ref[hd] * LOG2E
            m = jnp.maximum(jnp.max(s, axis=-1, keepdims=True), sink)
            p = jnp.exp2(s - m)
            denoms.append(jnp.sum(p, axis=-1, keepdims=True) + jnp.exp2(sink - m))
            ps.append(p.astype(bf16))
        o_all = jnp.dot(jnp.concatenate(ps, axis=0), v2, preferred_element_type=f32)
        for p in range(pairs):
            g = HEADS_PER_LANE_BLOCK * p
            o_lo = o_all[g * BLOCK:(g + 1) * BLOCK] / denoms[g]
            o_hi = o_all[(g + 1) * BLOCK:(g + 2) * BLOCK] / denoms[g + 1]
            c = h * pairs + p
            o_ref[0, :, c * LANES:(c + 1) * LANES] = jnp.where(low, o_lo, o_hi).astype(o_ref.dtype)

    s_next = scores(0)
    for h in range(n_kv):
        s_cur = s_next
        if h + 1 < n_kv:
            s_next = scores(h + 1)
        softmax_pv(h, s_cur)


def _swa_attention(qkv, sinks, *, n_heads, n_kv):
    b, s, _ = qkv.shape
    q_w = n_heads * HEAD_DIM
    kv_w = n_kv * HEAD_DIM
    assert q_w % kv_w == 0 and kv_w % LANES == 0
    k_col = q_w // kv_w
    v_col = k_col + 1
    slopes = tuple(2.0 ** (-8.0 * (h + 1.0) / n_heads) for h in range(n_heads))
    prev = lambda n: jnp.maximum(n - 1, 0)
    return pl.pallas_call(
        functools.partial(_swa_kernel, n_kv=n_kv, slopes=slopes),
        out_shape=jax.ShapeDtypeStruct((b, s, q_w), bf16),
        grid=(b, s // BLOCK),
        in_specs=[
            pl.BlockSpec(memory_space=pltpu.SMEM),
            pl.BlockSpec((1, BLOCK, q_w), lambda bi, n: (bi, n, 0)),
            pl.BlockSpec((1, BLOCK, kv_w), lambda bi, n: (bi, prev(n), k_col)),
            pl.BlockSpec((1, BLOCK, kv_w), lambda bi, n: (bi, n, k_col)),
            pl.BlockSpec((1, BLOCK, kv_w), lambda bi, n: (bi, prev(n), v_col)),
            pl.BlockSpec((1, BLOCK, kv_w), lambda bi, n: (bi, n, v_col)),
        ],
        out_specs=pl.BlockSpec((1, BLOCK, q_w), lambda bi, n: (bi, n, 0)),
        scratch_shapes=[pltpu.VMEM((2, n_heads, BLOCK, 2 * BLOCK), f32)],
        compiler_params=pltpu.CompilerParams(
            dimension_semantics=("arbitrary", "arbitrary"),
            vmem_limit_bytes=VMEM_LIMIT_BYTES),
        name="swa_attention",
    )(sinks, qkv, qkv, qkv, qkv, qkv)


def _sb_kernel(q_ref, k_ref, v_ref, o_ref, qm_ref, carry_ref, acc_ref, cmax_ref):
    qblk = pl.program_id(2)
    n_pairs = qm_ref.shape[0]
    rows = HEADS_PER_LANE_BLOCK * BLOCK
    row = lax.broadcasted_iota(jnp.int32, (rows, BLOCK), 0) % BLOCK
    col = lax.broadcasted_iota(jnp.int32, (rows, BLOCK), 1)
    before = col < row
    jj = lax.broadcasted_iota(jnp.int32, (2 * BLOCK, 2 * BLOCK), 0) % BLOCK
    ss = lax.broadcasted_iota(jnp.int32, (2 * BLOCK, 2 * BLOCK), 1)
    tri = jnp.where((ss >= BLOCK) | (jj > ss), 1.0, 0.0).astype(bf16)

    for c in range(n_pairs):
        qm_ref[c] = _split_head_pair(q_ref[0, :, c * LANES:(c + 1) * LANES])

    def sweep(j, *, n_blocks, diag):
        keys = n_blocks * BLOCK
        r = pl.multiple_of((j - (n_blocks - 1)) * BLOCK, BLOCK)
        ys = []
        for c in range(n_pairs):
            k = k_ref[0, pl.ds(r, keys), c * LANES:(c + 1) * LANES]
            z = lax.dot_general(qm_ref[c], k, NT_DIMS, preferred_element_type=f32)
            ys.append(z * (SCALE * LOG2E))
        log_betas, parts = [], []
        for y in ys:
            soft = jnp.log2(1.0 + jnp.exp2(-jnp.abs(y)))
            log_beta = jnp.minimum(y, 0.0) - soft
            log_1m = log_beta - y
            log_betas.append(log_beta)
            for kb in range(n_blocks):
                part = log_1m[:, kb * BLOCK:(kb + 1) * BLOCK]
                if diag and kb == n_blocks - 1:
                    part = jnp.where(before, part, 0.0)
                hi = part.astype(bf16)
                lo = (part - hi.astype(f32)).astype(bf16)
                parts.append(jnp.concatenate([hi, lo], axis=1))
        sums = jnp.dot(jnp.concatenate(parts, axis=0), tri, preferred_element_type=f32)
        cmax = None
        for c in range(n_pairs):
            run = None if diag else carry_ref[c]
            weights = [None] * n_blocks
            for kb in reversed(range(n_blocks)):
                s_kb = sums[(c * n_blocks + kb) * rows:(c * n_blocks + kb + 1) * rows]
                x = log_betas[c][:, kb * BLOCK:(kb + 1) * BLOCK] + s_kb[:, :BLOCK]
                if run is not None:
                    x = x + run
                a = jnp.exp2(x)
                if diag and kb == n_blocks - 1:
                    a = jnp.where(before, a, 0.0)
                weights[kb] = a.astype(bf16)
                total = s_kb[:, BLOCK:]
                run = total if run is None else run + total
            a_all = weights[0] if n_blocks == 1 else jnp.concatenate(weights, axis=1)
            v = v_ref[0, pl.ds(r, keys), c * LANES:(c + 1) * LANES]
            pv = jnp.dot(a_all, v, preferred_element_type=f32)
            if diag:
                acc_ref[c] = pv
            else:
                acc_ref[c] += pv
            carry_ref[c] = run
            cmax = run if cmax is None else jnp.maximum(cmax, run)
        return jnp.max(cmax)

    @pl.when(qblk == 0)
    def _():
        cmax_ref[0] = sweep(qblk, n_blocks=1, diag=True)

    @pl.when(qblk > 0)
    def _():
        cmax_ref[0] = sweep(qblk, n_blocks=2, diag=True)

    def cond(st):
        j, cmax = st
        return (j >= 0) & (cmax > LOG2_F32_UNDERFLOW)

    def body(st):
        j, _ = st
        return j - 1, sweep(j, n_blocks=1, diag=False)

    lax.while_loop(cond, body, (qblk - 2, cmax_ref[0]))

    low = lax.broadcasted_iota(jnp.int32, (BLOCK, LANES), 1) < HEAD_DIM
    for c in range(n_pairs):
        o_ref[0, :, c * LANES:(c + 1) * LANES] = jnp.where(
            low, acc_ref[c, :BLOCK], acc_ref[c, BLOCK:]).astype(o_ref.dtype)


def _sb_attention(qkv, *, n_heads):
    b, s, _ = qkv.shape
    w = n_heads * HEAD_DIM
    n_pairs = SB_HEADS // HEADS_PER_LANE_BLOCK
    width = SB_HEADS * HEAD_DIM
    assert w % width == 0
    n_col = w // width
    rows = HEADS_PER_LANE_BLOCK * BLOCK
    return pl.pallas_call(
        _sb_kernel,
        out_shape=jax.ShapeDtypeStruct((b, s, w), bf16),
        grid=(b, n_col, s // BLOCK),
        in_specs=[
            pl.BlockSpec((1, BLOCK, width), lambda bi, c, n: (bi, n, c)),
            pl.BlockSpec((1, s, width), lambda bi, c, n: (bi, 0, n_col + c)),
            pl.BlockSpec((1, s, width), lambda bi, c, n: (bi, 0, 2 * n_col + c)),
        ],
        out_specs=pl.BlockSpec((1, BLOCK, width), lambda bi, c, n: (bi, n, c)),
        scratch_shapes=[
            pltpu.VMEM((n_pairs, rows, LANES), bf16),
            pltpu.VMEM((n_pairs, rows, LANES), f32),
            pltpu.VMEM((n_pairs, rows, LANES), f32),
            pltpu.SMEM((1,), f32),
        ],
        compiler_params=pltpu.CompilerParams(
            dimension_semantics=("parallel", "parallel", "arbitrary"),
            vmem_limit_bytes=VMEM_LIMIT_BYTES),
        name="sb_attention",
    )(qkv, qkv, qkv)


def kernel(x, a_w_qkv, a_w_o, a_sinks, b_w_qkv, b_w_o, norm_mix, norm_mlp,
           mlp_w_in, mlp_w_out, final_norm):
    b, s, d = x.shape
    depth = norm_mix.shape[0]
    n_heads_a = a_w_o.shape[1] // HEAD_DIM
    n_kv_a = (a_w_qkv.shape[2] // HEAD_DIM - n_heads_a) // 2
    n_heads_b = b_w_o.shape[1] // HEAD_DIM
    assert n_heads_a == n_kv_a * GROUP_A and s % BLOCK == 0

    a_w_qkv, a_w_o, b_w_qkv, b_w_o, mlp_w_in, mlp_w_out = (
        w.astype(bf16) for w in (a_w_qkv, a_w_o, b_w_qkv, b_w_o, mlp_w_in, mlp_w_out))
    xf = x.reshape(b * s, d)
    final_gain = final_norm.reshape(1, d)
    for i in range(depth):
        j = i // N_MIXERS
        gain_mix = norm_mix[i].reshape(1, d)
        if i % N_MIXERS == 0:
            qkv = _norm_proj(xf, gain_mix, a_w_qkv, j, tm=1024, tn=1280)
            att = _swa_attention(qkv.reshape(b, s, -1), a_sinks[j],
                                 n_heads=n_heads_a, n_kv=n_kv_a)
            w_o = a_w_o
        else:
            qkv = _norm_proj(xf, gain_mix, b_w_qkv, j, tm=1024, tn=1024)
            att = _sb_attention(qkv.reshape(b, s, -1), n_heads=n_heads_b)
            w_o = b_w_o
        xf = _proj_res(att.reshape(b * s, -1), w_o, j, xf, tm=512)
        xf = _mlp(xf, norm_mlp[i].reshape(1, d), mlp_w_in, mlp_w_out, i, final_gain,
                  tm=1024, tf=1024, final_norm=(i == depth - 1))
    return xf.reshape(b, s, d)
```

```python
import functools
import math

import jax
import jax.numpy as jnp
from jax import lax
from jax.experimental import pallas as pl
from jax.experimental.pallas import tpu as pltpu

HEAD_DIM = 64
GROUP_A = 8
WINDOW = 128
BLOCK = 128
RMS_EPS = 1e-5
N_MIXERS = 2
LANES = 128
HEADS_PER_LANE_BLOCK = LANES // HEAD_DIM
MASK_VALUE = -1e30
LOG2E = math.log2(math.e)
LOG2_F32_UNDERFLOW = -150.0
SCALE = 1.0 / math.sqrt(HEAD_DIM)

VMEM_LIMIT_BYTES = 56 * 1024 * 1024
NORM_ROWS = 256
SB_HEADS = 16
SB_FIRST_BLOCKS = 3
EXP2_CLAMP = 126.0

bf16 = jnp.bfloat16
f32 = jnp.float32
NT_DIMS = (((1,), (1,)), ((), ()))


def _rmsnorm_rows(x, gain):
    y = x * lax.rsqrt(jnp.mean(x * x, axis=-1, keepdims=True) + RMS_EPS)
    return y * gain


def _store_normed(x_ref, gain_ref, xn_ref):
    rows = x_ref.shape[0]

    def step(c, _):
        r = pl.multiple_of(c * NORM_ROWS, NORM_ROWS)
        x = x_ref[pl.ds(r, NORM_ROWS), :]
        xn_ref[pl.ds(r, NORM_ROWS), :] = _rmsnorm_rows(x, gain_ref[...]).astype(xn_ref.dtype)
        return 0

    lax.fori_loop(0, rows // NORM_ROWS, step, 0)


def _norm_proj_kernel(x_ref, gain_ref, w_ref, o_ref, xn_ref):
    @pl.when(pl.program_id(1) == 0)
    def _():
        _store_normed(x_ref, gain_ref, xn_ref)

    o_ref[...] = jnp.dot(xn_ref[...], w_ref[...],
                         preferred_element_type=f32).astype(o_ref.dtype)


def _norm_proj(x, gain, w, layer, *, tm, tn):
    m, d = x.shape
    n = w.shape[2]
    assert m % tm == 0 and n % tn == 0
    return pl.pallas_call(
        _norm_proj_kernel,
        out_shape=jax.ShapeDtypeStruct((m, n), bf16),
        grid=(m // tm, n // tn),
        in_specs=[
            pl.BlockSpec((tm, d), lambda i, j: (i, 0)),
            pl.BlockSpec((1, d), lambda i, j: (0, 0)),
            pl.BlockSpec((None, d, tn), lambda i, j: (layer, 0, j)),
        ],
        out_specs=pl.BlockSpec((tm, tn), lambda i, j: (i, j)),
        scratch_shapes=[pltpu.VMEM((tm, d), bf16)],
        compiler_params=pltpu.CompilerParams(
            dimension_semantics=("parallel", "arbitrary"),
            vmem_limit_bytes=VMEM_LIMIT_BYTES),
        name="norm_proj",
    )(x, gain, w)


def _proj_res_kernel(a_ref, w_ref, r_ref, gain_ref, o_ref, xn_ref):
    y = r_ref[...] + jnp.dot(a_ref[...], w_ref[...], preferred_element_type=f32)
    o_ref[...] = y
    xn_ref[...] = _rmsnorm_rows(y, gain_ref[...]).astype(xn_ref.dtype)


def _proj_res(a, w, layer, res, gain, *, tm):
    m, k = a.shape
    n = w.shape[2]
    assert m % tm == 0
    return pl.pallas_call(
        _proj_res_kernel,
        out_shape=(jax.ShapeDtypeStruct((m, n), f32), jax.ShapeDtypeStruct((m, n), bf16)),
        grid=(m // tm,),
        in_specs=[
            pl.BlockSpec((tm, k), lambda i: (i, 0)),
            pl.BlockSpec((None, k, n), lambda i: (layer, 0, 0)),
            pl.BlockSpec((tm, n), lambda i: (i, 0)),
            pl.BlockSpec((1, n), lambda i: (0, 0)),
        ],
        out_specs=(pl.BlockSpec((tm, n), lambda i: (i, 0)),
                   pl.BlockSpec((tm, n), lambda i: (i, 0))),
        compiler_params=pltpu.CompilerParams(
            dimension_semantics=("parallel",),
            vmem_limit_bytes=VMEM_LIMIT_BYTES),
        name="proj_res",
    )(a, w, res, gain)


def _mlp_kernel(xn_ref, x_ref, win_ref, wout_ref, fgain_ref, o_ref, *, final_norm):
    f = pl.program_id(1)
    res_rows = x_ref.shape[0]

    @pl.when(f == 0)
    def _():
        def clear(c, _):
            r = pl.multiple_of(c * NORM_ROWS, NORM_ROWS)
            o_ref[pl.ds(r, NORM_ROWS), :] = jnp.zeros((NORM_ROWS, o_ref.shape[1]), f32)
            return 0
        lax.fori_loop(0, o_ref.shape[0] // NORM_ROWS, clear, 0)

    h = jnp.dot(xn_ref[...], win_ref[...], preferred_element_type=f32)
    h = jnp.maximum(h, 0.0)
    h = (h * h).astype(bf16)
    o_ref[...] += jnp.dot(h, wout_ref[...], preferred_element_type=f32)
    r = pl.multiple_of(f * res_rows, res_rows)
    o_ref[pl.ds(r, res_rows), :] += x_ref[...]

    if final_norm:
        @pl.when(f == pl.num_programs(1) - 1)
        def _():
            _store_normed(o_ref, fgain_ref, o_ref)


def _mlp(xn, x, w_in, w_out, layer, final_gain, *, tm, tf, final_norm):
    m, d = x.shape
    d_ff = w_in.shape[2]
    n_f = d_ff // tf
    assert m % tm == 0 and d_ff % tf == 0 and tm % n_f == 0
    return pl.pallas_call(
        functools.partial(_mlp_kernel, final_norm=final_norm),
        out_shape=jax.ShapeDtypeStruct((m, d), f32),
        grid=(m // tm, n_f),
        in_specs=[
            pl.BlockSpec((tm, d), lambda i, f: (i, 0)),
            pl.BlockSpec((tm // n_f, d), lambda i, f: (i * n_f + f, 0)),
            pl.BlockSpec((None, d, tf), lambda i, f: (layer, 0, f)),
            pl.BlockSpec((None, tf, d), lambda i, f: (layer, f, 0)),
            pl.BlockSpec((1, d), lambda i, f: (0, 0)),
        ],
        out_specs=pl.BlockSpec((tm, d), lambda i, f: (i, 0)),
        compiler_params=pltpu.CompilerParams(
            dimension_semantics=("parallel", "arbitrary"),
            vmem_limit_bytes=VMEM_LIMIT_BYTES),
        name="mlp_final" if final_norm else "mlp",
    )(xn, x, w_in, w_out, final_gain)


def _split_head_pair(qp):
    low = lax.broadcasted_iota(jnp.int32, qp.shape, 1) < HEAD_DIM
    zero = jnp.zeros_like(qp)
    return jnp.concatenate([jnp.where(low, qp, zero), jnp.where(low, zero, qp)], axis=0)


def _swa_kernel(sink_ref, q_ref, kp_ref, kc_ref, vp_ref, vc_ref, o_ref, bias_ref, *,
                n_kv, slopes):
    nblk = pl.program_id(1)
    n_heads = n_kv * GROUP_A

    @pl.when((pl.program_id(0) == 0) & (nblk == 0))
    def _():
        qi = lax.broadcasted_iota(jnp.int32, (BLOCK, 2 * BLOCK), 0)
        kj = lax.broadcasted_iota(jnp.int32, (BLOCK, 2 * BLOCK), 1)
        dist = qi + BLOCK - kj
        in_band = (dist >= 0) & (dist < WINDOW)
        distf = dist.astype(f32)
        for hd in range(n_heads):
            bias = jnp.where(in_band, (-slopes[hd] * LOG2E) * distf, MASK_VALUE)
            bias_ref[0, hd] = bias
            bias_ref[1, hd] = jnp.where(kj >= BLOCK, bias, MASK_VALUE)

    first = (nblk == 0).astype(jnp.int32)
    low = lax.broadcasted_iota(jnp.int32, (BLOCK, LANES), 1) < HEAD_DIM
    low_kv = lax.broadcasted_iota(jnp.int32, (2 * BLOCK, LANES), 1) < HEAD_DIM
    pairs = GROUP_A // HEADS_PER_LANE_BLOCK

    def band(prev_ref, cur_ref, h):
        c = h // HEADS_PER_LANE_BLOCK
        cols = slice(c * LANES, (c + 1) * LANES)
        x = jnp.concatenate([prev_ref[0, :, cols], cur_ref[0, :, cols]], axis=0)
        swapped = pltpu.roll(x, HEAD_DIM, axis=1)
        if h % HEADS_PER_LANE_BLOCK == 0:
            return jnp.where(low_kv, x, swapped)
        return jnp.where(low_kv, swapped, x)

    def scores(h):
        k2 = band(kp_ref, kc_ref, h)
        qs = []
        for p in range(pairs):
            c = h * pairs + p
            qs.append(_split_head_pair(q_ref[0, :, c * LANES:(c + 1) * LANES]))
        return lax.dot_general(jnp.concatenate(qs, axis=0), k2, NT_DIMS,
                               preferred_element_type=f32)

    def softmax_pv(h, s_all):
        v2 = band(vp_ref, vc_ref, h)
        ps, denoms = [], []
        for g in range(GROUP_A):
            hd = h * GROUP_A + g
            s = s_all[g * BLOCK:(g + 1) * BLOCK] + bias_ref[first, hd]
            sink = sink_ref[hd] * LOG2E
            m = jnp.maximum(jnp.max(s, axis=-1, keepdims=True), sink)
            p = jnp.exp2(s - m)
            denoms.append(jnp.sum(p, axis=-1, keepdims=True) + jnp.exp2(sink - m))
            ps.append(p.astype(bf16))
        o_all = jnp.dot(jnp.concatenate(ps, axis=0), v2, preferred_element_type=f32)
        for p in range(pairs):
            g = HEADS_PER_LANE_BLOCK * p
            o_lo = o_all[g * BLOCK:(g + 1) * BLOCK] / denoms[g]
            o_hi = o_all[(g + 1) * BLOCK:(g + 2) * BLOCK] / denoms[g + 1]
            c = h * pairs + p
            o_ref[0, :, c * LANES:(c + 1) * LANES] = jnp.where(low, o_lo, o_hi).astype(o_ref.dtype)

    s_next = scores(0)
    for h in range(n_kv):
        s_cur = s_next
        if h + 1 < n_kv:
            s_next = scores(h + 1)
        softmax_pv(h, s_cur)


def _swa_attention(qkv, sinks, *, n_heads, n_kv):
    b, s, _ = qkv.shape
    q_w = n_heads * HEAD_DIM
    kv_w = n_kv * HEAD_DIM
    assert q_w % kv_w == 0 and kv_w % LANES == 0
    k_col = q_w // kv_w
    v_col = k_col + 1
    slopes = tuple(2.0 ** (-8.0 * (h + 1.0) / n_heads) for h in range(n_heads))
    prev = lambda n: jnp.maximum(n - 1, 0)
    return pl.pallas_call(
        functools.partial(_swa_kernel, n_kv=n_kv, slopes=slopes),
        out_shape=jax.ShapeDtypeStruct((b, s, q_w), bf16),
        grid=(b, s // BLOCK),
        in_specs=[
            pl.BlockSpec(memory_space=pltpu.SMEM),
            pl.BlockSpec((1, BLOCK, q_w), lambda bi, n: (bi, n, 0)),
            pl.BlockSpec((1, BLOCK, kv_w), lambda bi, n: (bi, prev(n), k_col)),
            pl.BlockSpec((1, BLOCK, kv_w), lambda bi, n: (bi, n, k_col)),
            pl.BlockSpec((1, BLOCK, kv_w), lambda bi, n: (bi, prev(n), v_col)),
            pl.BlockSpec((1, BLOCK, kv_w), lambda bi, n: (bi, n, v_col)),
        ],
        out_specs=pl.BlockSpec((1, BLOCK, q_w), lambda bi, n: (bi, n, 0)),
        scratch_shapes=[pltpu.VMEM((2, n_heads, BLOCK, 2 * BLOCK), f32)],
        compiler_params=pltpu.CompilerParams(
            dimension_semantics=("arbitrary", "arbitrary"),
            vmem_limit_bytes=VMEM_LIMIT_BYTES),
        name="swa_attention",
    )(sinks, qkv, qkv, qkv, qkv, qkv)


def _sb_kernel(q_ref, k_ref, v_ref, o_ref, qm_ref, carry_ref, acc_ref, cmax_ref):
    qblk = pl.program_id(2)
    n_pairs = qm_ref.shape[0]
    rows = HEADS_PER_LANE_BLOCK * BLOCK
    row = lax.broadcasted_iota(jnp.int32, (rows, BLOCK), 0) % BLOCK
    col = lax.broadcasted_iota(jnp.int32, (rows, BLOCK), 1)
    before = col < row
    jj = lax.broadcasted_iota(jnp.int32, (2 * BLOCK, 2 * BLOCK), 0) % BLOCK
    ss = lax.broadcasted_iota(jnp.int32, (2 * BLOCK, 2 * BLOCK), 1)
    tri = jnp.where((ss >= BLOCK) | (jj > ss), 1.0, 0.0).astype(bf16)

    for c in range(n_pairs):
        qm_ref[c] = _split_head_pair(q_ref[0, :, c * LANES:(c + 1) * LANES])

    def sweep(j, *, n_blocks, diag):
        keys = n_blocks * BLOCK
        r = pl.multiple_of((j - (n_blocks - 1)) * BLOCK, BLOCK)
        ys = []
        for c in range(n_pairs):
            k = k_ref[0, pl.ds(r, keys), c * LANES:(c + 1) * LANES]
            ys.append(lax.dot_general(qm_ref[c], k, NT_DIMS, preferred_element_type=f32))
        log_betas, parts = [], []
        for y in ys:
            yc = jnp.minimum(y, EXP2_CLAMP)
            log_1m = jnp.log(1.0 + jnp.exp2(yc)) * (-LOG2E)
            log_beta = log_1m + yc
            log_betas.append(log_beta)
            for kb in range(n_blocks):
                part = log_1m[:, kb * BLOCK:(kb + 1) * BLOCK]
                if diag and kb == n_blocks - 1:
                    part = jnp.where(before, part, 0.0)
                hi = part.astype(bf16)
                lo = (part - hi.astype(f32)).astype(bf16)
                parts.append(jnp.concatenate([hi, lo], axis=1))
        sums = jnp.dot(jnp.concatenate(parts, axis=0), tri, preferred_element_type=f32)
        cmax = None
        for c in range(n_pairs):
            run = None if diag else carry_ref[c]
            weights = [None] * n_blocks
            for kb in reversed(range(n_blocks)):
                s_kb = sums[(c * n_blocks + kb) * rows:(c * n_blocks + kb + 1) * rows]
                x = log_betas[c][:, kb * BLOCK:(kb + 1) * BLOCK] + s_kb[:, :BLOCK]
                if run is not None:
                    x = x + run
                a = jnp.exp2(x)
                if diag and kb == n_blocks - 1:
                    a = jnp.where(before, a, 0.0)
                weights[kb] = a.astype(bf16)
                total = s_kb[:, BLOCK:]
                run = total if run is None else run + total
            a_all = weights[0] if n_blocks == 1 else jnp.concatenate(weights, axis=1)
            v = v_ref[0, pl.ds(r, keys), c * LANES:(c + 1) * LANES]
            pv = jnp.dot(a_all, v, preferred_element_type=f32)
            if diag:
                acc_ref[c] = pv
            else:
                acc_ref[c] += pv
            carry_ref[c] = run
            cmax = run if cmax is None else jnp.maximum(cmax, run)
        return jnp.max(cmax)

    for n_blocks in range(1, SB_FIRST_BLOCKS + 1):
        taken = (qblk == n_blocks - 1) if n_blocks < SB_FIRST_BLOCKS else (qblk >= n_blocks - 1)

        @pl.when(taken)
        def _(n_blocks=n_blocks):
            cmax_ref[0] = sweep(qblk, n_blocks=n_blocks, diag=True)

    def cond(st):
        j, cmax = st
        return (j >= 0) & (cmax > LOG2_F32_UNDERFLOW)

    def body(st):
        j, _ = st
        return j - 1, sweep(j, n_blocks=1, diag=False)

    lax.while_loop(cond, body, (qblk - SB_FIRST_BLOCKS, cmax_ref[0]))

    low = lax.broadcasted_iota(jnp.int32, (BLOCK, LANES), 1) < HEAD_DIM
    for c in range(n_pairs):
        o_ref[0, :, c * LANES:(c + 1) * LANES] = jnp.where(
            low, acc_ref[c, :BLOCK], acc_ref[c, BLOCK:]).astype(o_ref.dtype)


def _sb_attention(qkv, *, n_heads):
    b, s, _ = qkv.shape
    w = n_heads * HEAD_DIM
    n_pairs = SB_HEADS // HEADS_PER_LANE_BLOCK
    width = SB_HEADS * HEAD_DIM
    assert w % width == 0
    n_col = w // width
    rows = HEADS_PER_LANE_BLOCK * BLOCK
    return pl.pallas_call(
        _sb_kernel,
        out_shape=jax.ShapeDtypeStruct((b, s, w), bf16),
        grid=(b, n_col, s // BLOCK),
        in_specs=[
            pl.BlockSpec((1, BLOCK, width), lambda bi, c, n: (bi, n, c)),
            pl.BlockSpec((1, s, width), lambda bi, c, n: (bi, 0, n_col + c)),
            pl.BlockSpec((1, s, width), lambda bi, c, n: (bi, 0, 2 * n_col + c)),
        ],
        out_specs=pl.BlockSpec((1, BLOCK, width), lambda bi, c, n: (bi, n, c)),
        scratch_shapes=[
            pltpu.VMEM((n_pairs, rows, LANES), bf16),
            pltpu.VMEM((n_pairs, rows, LANES), f32),
            pltpu.VMEM((n_pairs, rows, LANES), f32),
            pltpu.SMEM((1,), f32),
        ],
        compiler_params=pltpu.CompilerParams(
            dimension_semantics=("parallel", "parallel", "arbitrary"),
            vmem_limit_bytes=VMEM_LIMIT_BYTES),
        name="sb_attention",
    )(qkv, qkv, qkv)


def kernel(x, a_w_qkv, a_w_o, a_sinks, b_w_qkv, b_w_o, norm_mix, norm_mlp,
           mlp_w_in, mlp_w_out, final_norm):
    b, s, d = x.shape
    depth = norm_mix.shape[0]
    n_heads_a = a_w_o.shape[1] // HEAD_DIM
    n_kv_a = (a_w_qkv.shape[2] // HEAD_DIM - n_heads_a) // 2
    n_heads_b = b_w_o.shape[1] // HEAD_DIM
    assert n_heads_a == n_kv_a * GROUP_A and s % BLOCK == 0

    def with_scaled_queries(w_qkv, q_width):
        col = lax.broadcasted_iota(jnp.int32, (1, 1, w_qkv.shape[2]), 2)
        return w_qkv * jnp.where(col < q_width, SCALE * LOG2E, 1.0).astype(f32)

    a_w_qkv = with_scaled_queries(a_w_qkv, n_heads_a * HEAD_DIM)
    b_w_qkv = with_scaled_queries(b_w_qkv, n_heads_b * HEAD_DIM)
    a_w_qkv, a_w_o, b_w_qkv, b_w_o, mlp_w_in, mlp_w_out = (
        w.astype(bf16) for w in (a_w_qkv, a_w_o, b_w_qkv, b_w_o, mlp_w_in, mlp_w_out))
    xf = x.reshape(b * s, d)
    final_gain = final_norm.reshape(1, d)
    for i in range(depth):
        j = i // N_MIXERS
        gain_mix = norm_mix[i].reshape(1, d)
        if i % N_MIXERS == 0:
            qkv = _norm_proj(xf, gain_mix, a_w_qkv, j, tm=1024, tn=1280)
            att = _swa_attention(qkv.reshape(b, s, -1), a_sinks[j],
                                 n_heads=n_heads_a, n_kv=n_kv_a)
            w_o = a_w_o
        else:
            qkv = _norm_proj(xf, gain_mix, b_w_qkv, j, tm=1024, tn=1024)
            att = _sb_attention(qkv.reshape(b, s, -1), n_heads=n_heads_b)
            w_o = b_w_o
        xf, xn = _proj_res(att.reshape(b * s, -1), w_o, j, xf, norm_mlp[i].reshape(1, d), tm=512)
        xf = _mlp(xn, xf, mlp_w_in, mlp_w_out, i, final_gain,
                  tm=1024, tf=1024, final_norm=(i == depth - 1))
    return xf.reshape(b, s, d)
```

```python
import functools
import math

import jax
import jax.numpy as jnp
from jax import lax
from jax.experimental import pallas as pl
from jax.experimental.pallas import tpu as pltpu

HEAD_DIM = 64
GROUP_A = 8
WINDOW = 128
BLOCK = 128
RMS_EPS = 1e-5
N_MIXERS = 2
LANES = 128
HEADS_PER_LANE_BLOCK = LANES // HEAD_DIM
MASK_VALUE = -1e30
LOG2E = math.log2(math.e)
LOG2_F32_UNDERFLOW = -150.0
SCALE = 1.0 / math.sqrt(HEAD_DIM)

VMEM_LIMIT_BYTES = 56 * 1024 * 1024
NORM_ROWS = 256
SB_HEADS = 16
SB_FIRST_BLOCKS = 3
EXP2_CLAMP = 126.0

bf16 = jnp.bfloat16
f32 = jnp.float32
NT_DIMS = (((1,), (1,)), ((), ()))


def _rmsnorm_rows(x, gain):
    y = x * lax.rsqrt(jnp.mean(x * x, axis=-1, keepdims=True) + RMS_EPS)
    return y * gain


def _store_normed(x_ref, gain_ref, xn_ref):
    rows = x_ref.shape[0]

    def step(c, _):
        r = pl.multiple_of(c * NORM_ROWS, NORM_ROWS)
        x = x_ref[pl.ds(r, NORM_ROWS), :]
        xn_ref[pl.ds(r, NORM_ROWS), :] = _rmsnorm_rows(x, gain_ref[...]).astype(xn_ref.dtype)
        return 0

    lax.fori_loop(0, rows // NORM_ROWS, step, 0)


def _norm_proj_kernel(x_ref, gain_ref, w_ref, o_ref, xn_ref):
    @pl.when(pl.program_id(1) == 0)
    def _():
        _store_normed(x_ref, gain_ref, xn_ref)

    o_ref[...] = jnp.dot(xn_ref[...], w_ref[...],
                         preferred_element_type=f32).astype(o_ref.dtype)


def _norm_proj(x, gain, w, layer, *, tm, tn):
    m, d = x.shape
    n = w.shape[2]
    assert m % tm == 0 and n % tn == 0
    return pl.pallas_call(
        _norm_proj_kernel,
        out_shape=jax.ShapeDtypeStruct((m, n), bf16),
        grid=(m // tm, n // tn),
        in_specs=[
            pl.BlockSpec((tm, d), lambda i, j: (i, 0)),
            pl.BlockSpec((1, d), lambda i, j: (0, 0)),
            pl.BlockSpec((None, d, tn), lambda i, j: (layer, 0, j)),
        ],
        out_specs=pl.BlockSpec((tm, tn), lambda i, j: (i, j)),
        scratch_shapes=[pltpu.VMEM((tm, d), bf16)],
        compiler_params=pltpu.CompilerParams(
            dimension_semantics=("parallel", "arbitrary"),
            vmem_limit_bytes=VMEM_LIMIT_BYTES),
        name="norm_proj",
    )(x, gain, w)


def _proj_res_kernel(a_ref, w_ref, r_ref, gain_ref, o_ref, xn_ref):
    y = r_ref[...] + jnp.dot(a_ref[...], w_ref[...], preferred_element_type=f32)
    o_ref[...] = y
    xn_ref[...] = _rmsnorm_rows(y, gain_ref[...]).astype(xn_ref.dtype)


def _proj_res(a, w, layer, res, gain, *, tm):
    m, k = a.shape
    n = w.shape[2]
    assert m % tm == 0
    return pl.pallas_call(
        _proj_res_kernel,
        out_shape=(jax.ShapeDtypeStruct((m, n), f32), jax.ShapeDtypeStruct((m, n), bf16)),
        grid=(m // tm,),
        in_specs=[
            pl.BlockSpec((tm, k), lambda i: (i, 0)),
            pl.BlockSpec((None, k, n), lambda i: (layer, 0, 0)),
            pl.BlockSpec((tm, n), lambda i: (i, 0)),
            pl.BlockSpec((1, n), lambda i: (0, 0)),
        ],
        out_specs=(pl.BlockSpec((tm, n), lambda i: (i, 0)),
                   pl.BlockSpec((tm, n), lambda i: (i, 0))),
        compiler_params=pltpu.CompilerParams(
            dimension_semantics=("parallel",),
            vmem_limit_bytes=VMEM_LIMIT_BYTES),
        name="proj_res",
    )(a, w, res, gain)


def _mlp_kernel(xn_ref, x_ref, win_ref, wout_ref, fgain_ref, o_ref, *, final_norm):
    f = pl.program_id(1)
    res_rows = x_ref.shape[0]

    @pl.when(f == 0)
    def _():
        def clear(c, _):
            r = pl.multiple_of(c * NORM_ROWS, NORM_ROWS)
            o_ref[pl.ds(r, NORM_ROWS), :] = jnp.zeros((NORM_ROWS, o_ref.shape[1]), f32)
            return 0
        lax.fori_loop(0, o_ref.shape[0] // NORM_ROWS, clear, 0)

    h = jnp.dot(xn_ref[...], win_ref[...], preferred_element_type=f32)
    h = jnp.maximum(h, 0.0)
    h = (h * h).astype(bf16)
    o_ref[...] += jnp.dot(h, wout_ref[...], preferred_element_type=f32)
    r = pl.multiple_of(f * res_rows, res_rows)
    o_ref[pl.ds(r, res_rows), :] += x_ref[...]

    if final_norm:
        @pl.when(f == pl.num_programs(1) - 1)
        def _():
            _store_normed(o_ref, fgain_ref, o_ref)


def _mlp(xn, x, w_in, w_out, layer, final_gain, *, tm, tf, final_norm):
    m, d = x.shape
    d_ff = w_in.shape[2]
    n_f = d_ff // tf
    assert m % tm == 0 and d_ff % tf == 0 and tm % n_f == 0
    return pl.pallas_call(
        functools.partial(_mlp_kernel, final_norm=final_norm),
        out_shape=jax.ShapeDtypeStruct((m, d), f32),
        grid=(m // tm, n_f),
        in_specs=[
            pl.BlockSpec((tm, d), lambda i, f: (i, 0)),
            pl.BlockSpec((tm // n_f, d), lambda i, f: (i * n_f + f, 0)),
            pl.BlockSpec((None, d, tf), lambda i, f: (layer, 0, f)),
            pl.BlockSpec((None, tf, d), lambda i, f: (layer, f, 0)),
            pl.BlockSpec((1, d), lambda i, f: (0, 0)),
        ],
        out_specs=pl.BlockSpec((tm, d), lambda i, f: (i, 0)),
        compiler_params=pltpu.CompilerParams(
            dimension_semantics=("parallel", "arbitrary"),
            vmem_limit_bytes=VMEM_LIMIT_BYTES),
        name="mlp_final" if final_norm else "mlp",
    )(xn, x, w_in, w_out, final_gain)


def _split_head_pair(qp):
    low = lax.broadcasted_iota(jnp.int32, qp.shape, 1) < HEAD_DIM
    zero = jnp.zeros_like(qp)
    return jnp.concatenate([jnp.where(low, qp, zero), jnp.where(low, zero, qp)], axis=0)


def _swa_kernel(sink_ref, q_ref, kp_ref, kc_ref, vp_ref, vc_ref, o_ref, bias_ref, *,
                n_kv, slopes):
    nblk = pl.program_id(1)
    n_heads = n_kv * GROUP_A

    @pl.when((pl.program_id(0) == 0) & (nblk == 0))
    def _():
        qi = lax.broadcasted_iota(jnp.int32, (BLOCK, 2 * BLOCK), 0)
        kj = lax.broadcasted_iota(jnp.int32, (BLOCK, 2 * BLOCK), 1)
        dist = qi + BLOCK - kj
        in_band = (dist >= 0) & (dist < WINDOW)
        distf = dist.astype(f32)
        for hd in range(n_heads):
            bias = jnp.where(in_band, (-slopes[hd] * LOG2E) * distf, MASK_VALUE)
            bias_ref[0, hd] = bias
            bias_ref[1, hd] = jnp.where(kj >= BLOCK, bias, MASK_VALUE)

    first = (nblk == 0).astype(jnp.int32)
    low = lax.broadcasted_iota(jnp.int32, (BLOCK, LANES), 1) < HEAD_DIM
    low_kv = lax.broadcasted_iota(jnp.int32, (2 * BLOCK, LANES), 1) < HEAD_DIM
    pairs = GROUP_A // HEADS_PER_LANE_BLOCK

    def band(prev_ref, cur_ref, h):
        c = h // HEADS_PER_LANE_BLOCK
        cols = slice(c * LANES, (c + 1) * LANES)
        x = jnp.concatenate([prev_ref[0, :, cols], cur_ref[0, :, cols]], axis=0)
        swapped = pltpu.roll(x, HEAD_DIM, axis=1)
        if h % HEADS_PER_LANE_BLOCK == 0:
            return jnp.where(low_kv, x, swapped)
        return jnp.where(low_kv, swapped, x)

    def scores(h):
        k2 = band(kp_ref, kc_ref, h)
        qs = []
        for p in range(pairs):
            c = h * pairs + p
            qs.append(_split_head_pair(q_ref[0, :, c * LANES:(c + 1) * LANES]))
        return lax.dot_general(jnp.concatenate(qs, axis=0), k2, NT_DIMS,
                               preferred_element_type=f32)

    def softmax_pv(h, s_all):
        v2 = band(vp_ref, vc_ref, h)
        ps, denoms = [], []
        for g in range(GROUP_A):
            hd = h * GROUP_A + g
            s = s_all[g * BLOCK:(g + 1) * BLOCK] + bias_ref[first, hd]
            sink = sink_ref[hd] * LOG2E
            m = jnp.maximum(jnp.max(s, axis=-1, keepdims=True), sink)
            p = jnp.exp2(s - m)
            denoms.append(jnp.sum(p, axis=-1, keepdims=True) + jnp.exp2(sink - m))
            ps.append(p.astype(bf16))
        o_all = jnp.dot(jnp.concatenate(ps, axis=0), v2, preferred_element_type=f32)
        for p in range(pairs):
            g = HEADS_PER_LANE_BLOCK * p
            o_lo = o_all[g * BLOCK:(g + 1) * BLOCK] / denoms[g]
            o_hi = o_all[(g + 1) * BLOCK:(g + 2) * BLOCK] / denoms[g + 1]
            c = h * pairs + p
            o_ref[0, :, c * LANES:(c + 1) * LANES] = jnp.where(low, o_lo, o_hi).astype(o_ref.dtype)

    s_next = scores(0)
    for h in range(n_kv):
        s_cur = s_next
        if h + 1 < n_kv:
            s_next = scores(h + 1)
        softmax_pv(h, s_cur)


def _swa_attention(qkv, sinks, *, n_heads, n_kv):
    b, s, _ = qkv.shape
    q_w = n_heads * HEAD_DIM
    kv_w = n_kv * HEAD_DIM
    assert q_w % kv_w == 0 and kv_w % LANES == 0
    k_col = q_w // kv_w
    v_col = k_col + 1
    slopes = tuple(2.0 ** (-8.0 * (h + 1.0) / n_heads) for h in range(n_heads))
    prev = lambda n: jnp.maximum(n - 1, 0)
    return pl.pallas_call(
        functools.partial(_swa_kernel, n_kv=n_kv, slopes=slopes),
        out_shape=jax.ShapeDtypeStruct((b, s, q_w), bf16),
        grid=(b, s // BLOCK),
        in_specs=[
            pl.BlockSpec(memory_space=pltpu.SMEM),
            pl.BlockSpec((1, BLOCK, q_w), lambda bi, n: (bi, n, 0)),
            pl.BlockSpec((1, BLOCK, kv_w), lambda bi, n: (bi, prev(n), k_col)),
            pl.BlockSpec((1, BLOCK, kv_w), lambda bi, n: (bi, n, k_col)),
            pl.BlockSpec((1, BLOCK, kv_w), lambda bi, n: (bi, prev(n), v_col)),
            pl.BlockSpec((1, BLOCK, kv_w), lambda bi, n: (bi, n, v_col)),
        ],
        out_specs=pl.BlockSpec((1, BLOCK, q_w), lambda bi, n: (bi, n, 0)),
        scratch_shapes=[pltpu.VMEM((2, n_heads, BLOCK, 2 * BLOCK), f32)],
        compiler_params=pltpu.CompilerParams(
            dimension_semantics=("arbitrary", "arbitrary"),
            vmem_limit_bytes=VMEM_LIMIT_BYTES),
        name="swa_attention",
    )(sinks, qkv, qkv, qkv, qkv, qkv)


def _sb_kernel(q_ref, k_ref, v_ref, o_ref, qm_ref, carry_ref, acc_ref, cmax_ref):
    qblk = pl.program_id(2)
    n_pairs = qm_ref.shape[0]
    rows = HEADS_PER_LANE_BLOCK * BLOCK
    row = lax.broadcasted_iota(jnp.int32, (rows, BLOCK), 0) % BLOCK
    col = lax.broadcasted_iota(jnp.int32, (rows, BLOCK), 1)
    before = col < row
    jj = lax.broadcasted_iota(jnp.int32, (BLOCK, BLOCK), 0)
    ss = lax.broadcasted_iota(jnp.int32, (BLOCK, BLOCK), 1)
    tri = jnp.where(jj > ss, 1.0, 0.0).astype(bf16)

    for c in range(n_pairs):
        qm_ref[c] = _split_head_pair(q_ref[0, :, c * LANES:(c + 1) * LANES])

    def sweep(j, *, n_blocks, diag):
        keys = n_blocks * BLOCK
        r = pl.multiple_of((j - (n_blocks - 1)) * BLOCK, BLOCK)
        ys = []
        for c in range(n_pairs):
            k = k_ref[0, pl.ds(r, keys), c * LANES:(c + 1) * LANES]
            ys.append(lax.dot_general(qm_ref[c], k, NT_DIMS, preferred_element_type=f32))
        log_betas, parts, totals = [], [], []
        for y in ys:
            yc = jnp.minimum(y, EXP2_CLAMP)
            log_1m = jnp.log(1.0 + jnp.exp2(yc)) * (-LOG2E)
            log_betas.append(log_1m + yc)
            for kb in range(n_blocks):
                part = log_1m[:, kb * BLOCK:(kb + 1) * BLOCK]
                if diag and kb == n_blocks - 1:
                    part = jnp.where(before, part, 0.0)
                totals.append(jnp.sum(part, axis=-1, keepdims=True))
                parts.append(part.astype(bf16))
        suffix = jnp.dot(jnp.concatenate(parts, axis=0), tri, preferred_element_type=f32)
        cmax = None
        for c in range(n_pairs):
            run = None if diag else carry_ref[c]
            weights = [None] * n_blocks
            for kb in reversed(range(n_blocks)):
                i_kb = c * n_blocks + kb
                x = (log_betas[c][:, kb * BLOCK:(kb + 1) * BLOCK]
                     + suffix[i_kb * rows:(i_kb + 1) * rows])
                if run is not None:
                    x = x + run
                a = jnp.exp2(x)
                if diag and kb == n_blocks - 1:
                    a = jnp.where(before, a, 0.0)
                weights[kb] = a.astype(bf16)
                run = totals[i_kb] if run is None else run + totals[i_kb]
            a_all = weights[0] if n_blocks == 1 else jnp.concatenate(weights, axis=1)
            v = v_ref[0, pl.ds(r, keys), c * LANES:(c + 1) * LANES]
            pv = jnp.dot(a_all, v, preferred_element_type=f32)
            if diag:
                acc_ref[c] = pv
            else:
                acc_ref[c] += pv
            carry_ref[c] = jnp.broadcast_to(run, (rows, LANES))
            cmax = run if cmax is None else jnp.maximum(cmax, run)
        return jnp.max(cmax)

    for n_blocks in range(1, SB_FIRST_BLOCKS + 1):
        taken = (qblk == n_blocks - 1) if n_blocks < SB_FIRST_BLOCKS else (qblk >= n_blocks - 1)

        @pl.when(taken)
        def _(n_blocks=n_blocks):
            cmax_ref[0] = sweep(qblk, n_blocks=n_blocks, diag=True)

    def cond(st):
        j, cmax = st
        return (j >= 0) & (cmax > LOG2_F32_UNDERFLOW)

    def body(st):
        j, _ = st
        return j - 1, sweep(j, n_blocks=1, diag=False)

    lax.while_loop(cond, body, (qblk - SB_FIRST_BLOCKS, cmax_ref[0]))

    low = lax.broadcasted_iota(jnp.int32, (BLOCK, LANES), 1) < HEAD_DIM
    for c in range(n_pairs):
        o_ref[0, :, c * LANES:(c + 1) * LANES] = jnp.where(
            low, acc_ref[c, :BLOCK], acc_ref[c, BLOCK:]).astype(o_ref.dtype)


def _sb_attention(qkv, *, n_heads):
    b, s, _ = qkv.shape
    w = n_heads * HEAD_DIM
    n_pairs = SB_HEADS // HEADS_PER_LANE_BLOCK
    width = SB_HEADS * HEAD_DIM
    assert w % width == 0
    n_col = w // width
    rows = HEADS_PER_LANE_BLOCK * BLOCK
    return pl.pallas_call(
        _sb_kernel,
        out_shape=jax.ShapeDtypeStruct((b, s, w), bf16),
        grid=(b, n_col, s // BLOCK),
        in_specs=[
            pl.BlockSpec((1, BLOCK, width), lambda bi, c, n: (bi, n, c)),
            pl.BlockSpec((1, s, width), lambda bi, c, n: (bi, 0, n_col + c)),
            pl.BlockSpec((1, s, width), lambda bi, c, n: (bi, 0, 2 * n_col + c)),
        ],
        out_specs=pl.BlockSpec((1, BLOCK, width), lambda bi, c, n: (bi, n, c)),
        scratch_shapes=[
            pltpu.VMEM((n_pairs, rows, LANES), bf16),
            pltpu.VMEM((n_pairs, rows, LANES), f32),
            pltpu.VMEM((n_pairs, rows, LANES), f32),
            pltpu.SMEM((1,), f32),
        ],
        compiler_params=pltpu.CompilerParams(
            dimension_semantics=("parallel", "parallel", "arbitrary"),
            vmem_limit_bytes=VMEM_LIMIT_BYTES),
        name="sb_attention",
    )(qkv, qkv, qkv)


def kernel(x, a_w_qkv, a_w_o, a_sinks, b_w_qkv, b_w_o, norm_mix, norm_mlp,
           mlp_w_in, mlp_w_out, final_norm):
    b, s, d = x.shape
    depth = norm_mix.shape[0]
    n_heads_a = a_w_o.shape[1] // HEAD_DIM
    n_kv_a = (a_w_qkv.shape[2] // HEAD_DIM - n_heads_a) // 2
    n_heads_b = b_w_o.shape[1] // HEAD_DIM
    assert n_heads_a == n_kv_a * GROUP_A and s % BLOCK == 0

    def with_scaled_queries(w_qkv, q_width):
        col = lax.broadcasted_iota(jnp.int32, (1, 1, w_qkv.shape[2]), 2)
        return w_qkv * jnp.where(col < q_width, SCALE * LOG2E, 1.0).astype(f32)

    a_w_qkv = with_scaled_queries(a_w_qkv, n_heads_a * HEAD_DIM)
    b_w_qkv = with_scaled_queries(b_w_qkv, n_heads_b * HEAD_DIM)
    a_w_qkv, a_w_o, b_w_qkv, b_w_o, mlp_w_in, mlp_w_out = (
        w.astype(bf16) for w in (a_w_qkv, a_w_o, b_w_qkv, b_w_o, mlp_w_in, mlp_w_out))
    xf = x.reshape(b * s, d)
    final_gain = final_norm.reshape(1, d)
    for i in range(depth):
        j = i // N_MIXERS
        gain_mix = norm_mix[i].reshape(1, d)
        if i % N_MIXERS == 0:
            qkv = _norm_proj(xf, gain_mix, a_w_qkv, j, tm=1024, tn=1280)
            att = _swa_attention(qkv.reshape(b, s, -1), a_sinks[j],
                                 n_heads=n_heads_a, n_kv=n_kv_a)
            w_o = a_w_o
        else:
            qkv = _norm_proj(xf, gain_mix, b_w_qkv, j, tm=1024, tn=1024)
            att = _sb_attention(qkv.reshape(b, s, -1), n_heads=n_heads_b)
            w_o = b_w_o
        xf, xn = _proj_res(att.reshape(b * s, -1), w_o, j, xf, norm_mlp[i].reshape(1, d), tm=512)
        xf = _mlp(xn, xf, mlp_w_in, mlp_w_out, i, final_gain,
                  tm=1024, tf=1024, final_norm=(i == depth - 1))
    return xf.reshape(b, s, d)
```

```python
import functools
import math

import jax
import jax.numpy as jnp
from jax import lax
from jax.experimental import pallas as pl
from jax.experimental.pallas import tpu as pltpu

HEAD_DIM = 64
GROUP_A = 8
WINDOW = 128
BLOCK = 128
RMS_EPS = 1e-5
N_MIXERS = 2
LANES = 128
HEADS_PER_LANE_BLOCK = LANES // HEAD_DIM
MASK_VALUE = -1e30
LOG2E = math.log2(math.e)
LOG2_F32_UNDERFLOW = -150.0
SCALE = 1.0 / math.sqrt(HEAD_DIM)

VMEM_LIMIT_BYTES = 56 * 1024 * 1024
NORM_ROWS = 256
SB_HEADS = 16
SB_FIRST_BLOCKS = 3
EXP2_CLAMP = 126.0

bf16 = jnp.bfloat16
f32 = jnp.float32
NT_DIMS = (((1,), (1,)), ((), ()))


def _rmsnorm_rows(x, gain):
    y = x * lax.rsqrt(jnp.mean(x * x, axis=-1, keepdims=True) + RMS_EPS)
    return y * gain


def _store_normed(x_ref, gain_ref, xn_ref):
    rows = x_ref.shape[0]

    def step(c, _):
        r = pl.multiple_of(c * NORM_ROWS, NORM_ROWS)
        x = x_ref[pl.ds(r, NORM_ROWS), :]
        xn_ref[pl.ds(r, NORM_ROWS), :] = _rmsnorm_rows(x, gain_ref[...]).astype(xn_ref.dtype)
        return 0

    lax.fori_loop(0, rows // NORM_ROWS, step, 0)


def _norm_proj_kernel(x_ref, gain_ref, w_ref, o_ref, xn_ref):
    @pl.when(pl.program_id(1) == 0)
    def _():
        _store_normed(x_ref, gain_ref, xn_ref)

    o_ref[...] = jnp.dot(xn_ref[...], w_ref[...],
                         preferred_element_type=f32).astype(o_ref.dtype)


def _norm_proj(x, gain, w, layer, *, tm, tn):
    m, d = x.shape
    n = w.shape[2]
    assert m % tm == 0 and n % tn == 0
    return pl.pallas_call(
        _norm_proj_kernel,
        out_shape=jax.ShapeDtypeStruct((m, n), bf16),
        grid=(m // tm, n // tn),
        in_specs=[
            pl.BlockSpec((tm, d), lambda i, j: (i, 0)),
            pl.BlockSpec((1, d), lambda i, j: (0, 0)),
            pl.BlockSpec((None, d, tn), lambda i, j: (layer, 0, j)),
        ],
        out_specs=pl.BlockSpec((tm, tn), lambda i, j: (i, j)),
        scratch_shapes=[pltpu.VMEM((tm, d), bf16)],
        compiler_params=pltpu.CompilerParams(
            dimension_semantics=("parallel", "arbitrary"),
            vmem_limit_bytes=VMEM_LIMIT_BYTES),
        name="norm_proj",
    )(x, gain, w)


def _proj_res_kernel(a_ref, w_ref, r_ref, gain_ref, o_ref, xn_ref):
    y = r_ref[...] + jnp.dot(a_ref[...], w_ref[...], preferred_element_type=f32)
    o_ref[...] = y
    xn_ref[...] = _rmsnorm_rows(y, gain_ref[...]).astype(xn_ref.dtype)


def _proj_res(a, w, layer, res, gain, *, tm):
    m, k = a.shape
    n = w.shape[2]
    assert m % tm == 0
    return pl.pallas_call(
        _proj_res_kernel,
        out_shape=(jax.ShapeDtypeStruct((m, n), f32), jax.ShapeDtypeStruct((m, n), bf16)),
        grid=(m // tm,),
        in_specs=[
            pl.BlockSpec((tm, k), lambda i: (i, 0)),
            pl.BlockSpec((None, k, n), lambda i: (layer, 0, 0)),
            pl.BlockSpec((tm, n), lambda i: (i, 0)),
            pl.BlockSpec((1, n), lambda i: (0, 0)),
        ],
        out_specs=(pl.BlockSpec((tm, n), lambda i: (i, 0)),
                   pl.BlockSpec((tm, n), lambda i: (i, 0))),
        compiler_params=pltpu.CompilerParams(
            dimension_semantics=("parallel",),
            vmem_limit_bytes=VMEM_LIMIT_BYTES),
        name="proj_res",
    )(a, w, res, gain)


def _mlp_kernel(xn_ref, x_ref, win_ref, wout_ref, fgain_ref, o_ref, *, final_norm):
    f = pl.program_id(1)
    res_rows = x_ref.shape[0]

    @pl.when(f == 0)
    def _():
        def clear(c, _):
            r = pl.multiple_of(c * NORM_ROWS, NORM_ROWS)
            o_ref[pl.ds(r, NORM_ROWS), :] = jnp.zeros((NORM_ROWS, o_ref.shape[1]), f32)
            return 0
        lax.fori_loop(0, o_ref.shape[0] // NORM_ROWS, clear, 0)

    h = jnp.dot(xn_ref[...], win_ref[...], preferred_element_type=f32)
    h = jnp.maximum(h, 0.0)
    h = (h * h).astype(bf16)
    o_ref[...] += jnp.dot(h, wout_ref[...], preferred_element_type=f32)
    r = pl.multiple_of(f * res_rows, res_rows)
    o_ref[pl.ds(r, res_rows), :] += x_ref[...]

    if final_norm:
        @pl.when(f == pl.num_programs(1) - 1)
        def _():
            _store_normed(o_ref, fgain_ref, o_ref)


def _mlp(xn, x, w_in, w_out, layer, final_gain, *, tm, tf, final_norm):
    m, d = x.shape
    d_ff = w_in.shape[2]
    n_f = d_ff // tf
    assert m % tm == 0 and d_ff % tf == 0 and tm % n_f == 0
    return pl.pallas_call(
        functools.partial(_mlp_kernel, final_norm=final_norm),
        out_shape=jax.ShapeDtypeStruct((m, d), f32),
        grid=(m // tm, n_f),
        in_specs=[
            pl.BlockSpec((tm, d), lambda i, f: (i, 0)),
            pl.BlockSpec((tm // n_f, d), lambda i, f: (i * n_f + f, 0)),
            pl.BlockSpec((None, d, tf), lambda i, f: (layer, 0, f)),
            pl.BlockSpec((None, tf, d), lambda i, f: (layer, f, 0)),
            pl.BlockSpec((1, d), lambda i, f: (0, 0)),
        ],
        out_specs=pl.BlockSpec((tm, d), lambda i, f: (i, 0)),
        compiler_params=pltpu.CompilerParams(
            dimension_semantics=("parallel", "arbitrary"),
            vmem_limit_bytes=VMEM_LIMIT_BYTES),
        name="mlp_final" if final_norm else "mlp",
    )(xn, x, w_in, w_out, final_gain)


def _split_head_pair(qp):
    low = lax.broadcasted_iota(jnp.int32, qp.shape, 1) < HEAD_DIM
    zero = jnp.zeros_like(qp)
    return jnp.concatenate([jnp.where(low, qp, zero), jnp.where(low, zero, qp)], axis=0)


def _swa_kernel(sink_ref, q_ref, kp_ref, kc_ref, vp_ref, vc_ref, o_ref, bias_ref, *,
                n_kv, slopes):
    nblk = pl.program_id(1)
    n_heads = n_kv * GROUP_A

    @pl.when((pl.program_id(0) == 0) & (nblk == 0))
    def _():
        kj = lax.broadcasted_iota(jnp.int32, (2 * BLOCK, BLOCK), 0)
        qi = lax.broadcasted_iota(jnp.int32, (2 * BLOCK, BLOCK), 1)
        dist = qi + BLOCK - kj
        in_band = (dist >= 0) & (dist < WINDOW)
        distf = dist.astype(f32)
        for hd in range(n_heads):
            bias = jnp.where(in_band, (-slopes[hd] * LOG2E) * distf, MASK_VALUE)
            bias_ref[0, hd] = bias
            bias_ref[1, hd] = jnp.where(kj >= BLOCK, bias, MASK_VALUE)

    first = (nblk == 0).astype(jnp.int32)
    low_kv = lax.broadcasted_iota(jnp.int32, (2 * BLOCK, LANES), 1) < HEAD_DIM
    pairs = GROUP_A // HEADS_PER_LANE_BLOCK

    def band(prev_ref, cur_ref, h):
        c = h // HEADS_PER_LANE_BLOCK
        cols = slice(c * LANES, (c + 1) * LANES)
        return jnp.concatenate([prev_ref[0, :, cols], cur_ref[0, :, cols]], axis=0)

    def scores(h):
        x = band(kp_ref, kc_ref, h)
        swapped = pltpu.roll(x, HEAD_DIM, axis=1)
        if h % HEADS_PER_LANE_BLOCK == 0:
            k2 = jnp.where(low_kv, x, swapped)
        else:
            k2 = jnp.where(low_kv, swapped, x)
        qs = []
        for p in range(pairs):
            c = h * pairs + p
            qs.append(_split_head_pair(q_ref[0, :, c * LANES:(c + 1) * LANES]))
        return lax.dot_general(k2, jnp.concatenate(qs, axis=0), NT_DIMS,
                               preferred_element_type=f32)

    def softmax_pv(h, s_all):
        lo = (h % HEADS_PER_LANE_BLOCK) * HEAD_DIM
        v_t = band(vp_ref, vc_ref, h).astype(f32).T[lo:lo + HEAD_DIM].astype(bf16)
        ps, inv_denoms = [], []
        for g in range(GROUP_A):
            hd = h * GROUP_A + g
            s = s_all[:, g * BLOCK:(g + 1) * BLOCK] + bias_ref[first, hd]
            sink = sink_ref[hd] * LOG2E
            m = jnp.maximum(jnp.max(s, axis=0, keepdims=True), sink)
            p = jnp.exp2(s - m)
            inv_denoms.append(1.0 / (jnp.sum(p, axis=0, keepdims=True) + jnp.exp2(sink - m)))
            ps.append(p.astype(bf16))
        o_t = jnp.dot(v_t, jnp.concatenate(ps, axis=1), preferred_element_type=f32)
        for p in range(pairs):
            g = HEADS_PER_LANE_BLOCK * p
            pair_t = jnp.concatenate(
                [o_t[:, (g + i) * BLOCK:(g + i + 1) * BLOCK] * inv_denoms[g + i]
                 for i in range(HEADS_PER_LANE_BLOCK)], axis=0)
            c = h * pairs + p
            o_ref[0, :, c * LANES:(c + 1) * LANES] = pair_t.T.astype(o_ref.dtype)

    s_next = scores(0)
    for h in range(n_kv):
        s_cur = s_next
        if h + 1 < n_kv:
            s_next = scores(h + 1)
        softmax_pv(h, s_cur)


def _swa_attention(qkv, sinks, *, n_heads, n_kv):
    b, s, _ = qkv.shape
    q_w = n_heads * HEAD_DIM
    kv_w = n_kv * HEAD_DIM
    assert q_w % kv_w == 0 and kv_w % LANES == 0
    k_col = q_w // kv_w
    v_col = k_col + 1
    slopes = tuple(2.0 ** (-8.0 * (h + 1.0) / n_heads) for h in range(n_heads))
    prev = lambda n: jnp.maximum(n - 1, 0)
    return pl.pallas_call(
        functools.partial(_swa_kernel, n_kv=n_kv, slopes=slopes),
        out_shape=jax.ShapeDtypeStruct((b, s, q_w), bf16),
        grid=(b, s // BLOCK),
        in_specs=[
            pl.BlockSpec(memory_space=pltpu.SMEM),
            pl.BlockSpec((1, BLOCK, q_w), lambda bi, n: (bi, n, 0)),
            pl.BlockSpec((1, BLOCK, kv_w), lambda bi, n: (bi, prev(n), k_col)),
            pl.BlockSpec((1, BLOCK, kv_w), lambda bi, n: (bi, n, k_col)),
            pl.BlockSpec((1, BLOCK, kv_w), lambda bi, n: (bi, prev(n), v_col)),
            pl.BlockSpec((1, BLOCK, kv_w), lambda bi, n: (bi, n, v_col)),
        ],
        out_specs=pl.BlockSpec((1, BLOCK, q_w), lambda bi, n: (bi, n, 0)),
        scratch_shapes=[pltpu.VMEM((2, n_heads, 2 * BLOCK, BLOCK), f32)],
        compiler_params=pltpu.CompilerParams(
            dimension_semantics=("arbitrary", "arbitrary"),
            vmem_limit_bytes=VMEM_LIMIT_BYTES),
        name="swa_attention",
    )(sinks, qkv, qkv, qkv, qkv, qkv)


def _sb_kernel(q_ref, k_ref, v_ref, o_ref, qm_ref, carry_ref, acc_ref, cmax_ref):
    qblk = pl.program_id(2)
    n_pairs = qm_ref.shape[0]
    rows = HEADS_PER_LANE_BLOCK * BLOCK
    row = lax.broadcasted_iota(jnp.int32, (rows, BLOCK), 0) % BLOCK
    col = lax.broadcasted_iota(jnp.int32, (rows, BLOCK), 1)
    before = col < row
    jj = lax.broadcasted_iota(jnp.int32, (BLOCK, BLOCK), 0)
    ss = lax.broadcasted_iota(jnp.int32, (BLOCK, BLOCK), 1)
    tri = jnp.where(jj > ss, 1.0, 0.0).astype(bf16)

    for c in range(n_pairs):
        qm_ref[c] = _split_head_pair(q_ref[0, :, c * LANES:(c + 1) * LANES])

    def sweep(j, *, n_blocks, diag):
        keys = n_blocks * BLOCK
        r = pl.multiple_of((j - (n_blocks - 1)) * BLOCK, BLOCK)
        ys = []
        for c in range(n_pairs):
            k = k_ref[0, pl.ds(r, keys), c * LANES:(c + 1) * LANES]
            ys.append(lax.dot_general(qm_ref[c], k, NT_DIMS, preferred_element_type=f32))
        log_betas, parts, totals = [], [], []
        for y in ys:
            yc = jnp.minimum(y, EXP2_CLAMP)
            log_1m = jnp.log(1.0 + jnp.exp2(yc)) * (-LOG2E)
            log_betas.append(log_1m + yc)
            for kb in range(n_blocks):
                part = log_1m[:, kb * BLOCK:(kb + 1) * BLOCK]
                if diag and kb == n_blocks - 1:
                    part = jnp.where(before, part, 0.0)
                totals.append(jnp.sum(part, axis=-1, keepdims=True))
                parts.append(part.astype(bf16))
        suffix = jnp.dot(jnp.concatenate(parts, axis=0), tri, preferred_element_type=f32)
        cmax = None
        for c in range(n_pairs):
            run = None if diag else carry_ref[c]
            weights = [None] * n_blocks
            for kb in reversed(range(n_blocks)):
                i_kb = c * n_blocks + kb
                x = (log_betas[c][:, kb * BLOCK:(kb + 1) * BLOCK]
                     + suffix[i_kb * rows:(i_kb + 1) * rows])
                if run is not None:
                    x = x + run
                a = jnp.exp2(x)
                if diag and kb == n_blocks - 1:
                    a = jnp.where(before, a, 0.0)
                weights[kb] = a.astype(bf16)
                run = totals[i_kb] if run is None else run + totals[i_kb]
            a_all = weights[0] if n_blocks == 1 else jnp.concatenate(weights, axis=1)
            v = v_ref[0, pl.ds(r, keys), c * LANES:(c + 1) * LANES]
            pv = jnp.dot(a_all, v, preferred_element_type=f32)
            if diag:
                acc_ref[c] = pv
            else:
                acc_ref[c] += pv
            carry_ref[c] = jnp.broadcast_to(run, (rows, LANES))
            cmax = run if cmax is None else jnp.maximum(cmax, run)
        return jnp.max(cmax)

    for n_blocks in range(1, SB_FIRST_BLOCKS + 1):
        taken = (qblk == n_blocks - 1) if n_blocks < SB_FIRST_BLOCKS else (qblk >= n_blocks - 1)

        @pl.when(taken)
        def _(n_blocks=n_blocks):
            cmax_ref[0] = sweep(qblk, n_blocks=n_blocks, diag=True)

    def cond(st):
        j, cmax = st
        return (j >= 0) & (cmax > LOG2_F32_UNDERFLOW)

    def body(st):
        j, _ = st
        return j - 1, sweep(j, n_blocks=1, diag=False)

    lax.while_loop(cond, body, (qblk - SB_FIRST_BLOCKS, cmax_ref[0]))

    low = lax.broadcasted_iota(jnp.int32, (BLOCK, LANES), 1) < HEAD_DIM
    for c in range(n_pairs):
        o_ref[0, :, c * LANES:(c + 1) * LANES] = jnp.where(
            low, acc_ref[c, :BLOCK], acc_ref[c, BLOCK:]).astype(o_ref.dtype)


def _sb_attention(qkv, *, n_heads):
    b, s, _ = qkv.shape
    w = n_heads * HEAD_DIM
    n_pairs = SB_HEADS // HEADS_PER_LANE_BLOCK
    width = SB_HEADS * HEAD_DIM
    assert w % width == 0
    n_col = w // width
    rows = HEADS_PER_LANE_BLOCK * BLOCK
    return pl.pallas_call(
        _sb_kernel,
        out_shape=jax.ShapeDtypeStruct((b, s, w), bf16),
        grid=(b, n_col, s // BLOCK),
        in_specs=[
            pl.BlockSpec((1, BLOCK, width), lambda bi, c, n: (bi, n, c)),
            pl.BlockSpec((1, s, width), lambda bi, c, n: (bi, 0, n_col + c)),
            pl.BlockSpec((1, s, width), lambda bi, c, n: (bi, 0, 2 * n_col + c)),
        ],
        out_specs=pl.BlockSpec((1, BLOCK, width), lambda bi, c, n: (bi, n, c)),
        scratch_shapes=[
            pltpu.VMEM((n_pairs, rows, LANES), bf16),
            pltpu.VMEM((n_pairs, rows, LANES), f32),
            pltpu.VMEM((n_pairs, rows, LANES), f32),
            pltpu.SMEM((1,), f32),
        ],
        compiler_params=pltpu.CompilerParams(
            dimension_semantics=("parallel", "parallel", "arbitrary"),
            vmem_limit_bytes=VMEM_LIMIT_BYTES),
        name="sb_attention",
    )(qkv, qkv, qkv)


def kernel(x, a_w_qkv, a_w_o, a_sinks, b_w_qkv, b_w_o, norm_mix, norm_mlp,
           mlp_w_in, mlp_w_out, final_norm):
    b, s, d = x.shape
    depth = norm_mix.shape[0]
    n_heads_a = a_w_o.shape[1] // HEAD_DIM
    n_kv_a = (a_w_qkv.shape[2] // HEAD_DIM - n_heads_a) // 2
    n_heads_b = b_w_o.shape[1] // HEAD_DIM
    assert n_heads_a == n_kv_a * GROUP_A and s % BLOCK == 0

    def with_scaled_queries(w_qkv, q_width):
        col = lax.broadcasted_iota(jnp.int32, (1, 1, w_qkv.shape[2]), 2)
        return w_qkv * jnp.where(col < q_width, SCALE * LOG2E, 1.0).astype(f32)

    a_w_qkv = with_scaled_queries(a_w_qkv, n_heads_a * HEAD_DIM)
    b_w_qkv = with_scaled_queries(b_w_qkv, n_heads_b * HEAD_DIM)
    a_w_qkv, a_w_o, b_w_qkv, b_w_o, mlp_w_in, mlp_w_out = (
        w.astype(bf16) for w in (a_w_qkv, a_w_o, b_w_qkv, b_w_o, mlp_w_in, mlp_w_out))
    xf = x.reshape(b * s, d)
    final_gain = final_norm.reshape(1, d)
    for i in range(depth):
        j = i // N_MIXERS
        gain_mix = norm_mix[i].reshape(1, d)
        if i % N_MIXERS == 0:
            qkv = _norm_proj(xf, gain_mix, a_w_qkv, j, tm=1024, tn=1280)
            att = _swa_attention(qkv.reshape(b, s, -1), a_sinks[j],
                                 n_heads=n_heads_a, n_kv=n_kv_a)
            w_o = a_w_o
        else:
            qkv = _norm_proj(xf, gain_mix, b_w_qkv, j, tm=1024, tn=1024)
            att = _sb_attention(qkv.reshape(b, s, -1), n_heads=n_heads_b)
            w_o = b_w_o
        xf, xn = _proj_res(att.reshape(b * s, -1), w_o, j, xf, norm_mlp[i].reshape(1, d), tm=512)
        xf = _mlp(xn, xf, mlp_w_in, mlp_w_out, i, final_gain,
                  tm=1024, tf=1024, final_norm=(i == depth - 1))
    return xf.reshape(b, s, d)
```

```python
import functools
import math

import jax
import jax.numpy as jnp
from jax import lax
from jax.experimental import pallas as pl
from jax.experimental.pallas import tpu as pltpu

HEAD_DIM = 64
GROUP_A = 8
WINDOW = 128
BLOCK = 128
RMS_EPS = 1e-5
N_MIXERS = 2
LANES = 128
HEADS_PER_LANE_BLOCK = LANES // HEAD_DIM
MASK_VALUE = -1e30
LOG2E = math.log2(math.e)
LOG2_F32_UNDERFLOW = -150.0
SCALE = 1.0 / math.sqrt(HEAD_DIM)

VMEM_LIMIT_BYTES = 56 * 1024 * 1024
NORM_ROWS = 256
SB_HEADS = 16
SB_FIRST_BLOCKS = 3
EXP2_CLAMP = 126.0
SB_TOP_ROWS = 64

bf16 = jnp.bfloat16
f32 = jnp.float32
NT_DIMS = (((1,), (1,)), ((), ()))


def _rmsnorm_rows(x, gain):
    y = x * lax.rsqrt(jnp.mean(x * x, axis=-1, keepdims=True) + RMS_EPS)
    return y * gain


def _store_normed(x_ref, gain_ref, xn_ref):
    rows = x_ref.shape[0]

    def step(c, _):
        r = pl.multiple_of(c * NORM_ROWS, NORM_ROWS)
        x = x_ref[pl.ds(r, NORM_ROWS), :]
        xn_ref[pl.ds(r, NORM_ROWS), :] = _rmsnorm_rows(x, gain_ref[...]).astype(xn_ref.dtype)
        return 0

    lax.fori_loop(0, rows // NORM_ROWS, step, 0)


def _norm_proj_kernel(x_ref, w_ref, o_ref, xb_ref, rinv_ref):
    @pl.when(pl.program_id(1) == 0)
    def _():
        def step(c, _):
            rows = pl.ds(pl.multiple_of(c * NORM_ROWS, NORM_ROWS), NORM_ROWS)
            x = x_ref[rows, :]
            xb_ref[rows, :] = x.astype(xb_ref.dtype)
            rinv_ref[rows, :] = lax.rsqrt(jnp.mean(x * x, axis=-1, keepdims=True) + RMS_EPS)
            return 0
        lax.fori_loop(0, x_ref.shape[0] // NORM_ROWS, step, 0)

    y = jnp.dot(xb_ref[...], w_ref[...], preferred_element_type=f32)
    o_ref[...] = (y * rinv_ref[...]).astype(o_ref.dtype)


def _norm_proj(x, w, layer, *, tm, tn):
    m, d = x.shape
    n = w.shape[2]
    assert m % tm == 0 and n % tn == 0
    return pl.pallas_call(
        _norm_proj_kernel,
        out_shape=jax.ShapeDtypeStruct((m, n), bf16),
        grid=(m // tm, n // tn),
        in_specs=[
            pl.BlockSpec((tm, d), lambda i, j: (i, 0)),
            pl.BlockSpec((None, d, tn), lambda i, j: (layer, 0, j)),
        ],
        out_specs=pl.BlockSpec((tm, tn), lambda i, j: (i, j)),
        scratch_shapes=[pltpu.VMEM((tm, d), bf16), pltpu.VMEM((tm, 1), f32)],
        compiler_params=pltpu.CompilerParams(
            dimension_semantics=("parallel", "arbitrary"),
            vmem_limit_bytes=VMEM_LIMIT_BYTES),
        name="norm_proj",
    )(x, w)


def _proj_res_kernel(a_ref, w_ref, r_ref, gain_ref, o_ref, xn_ref):
    y = r_ref[...] + jnp.dot(a_ref[...], w_ref[...], preferred_element_type=f32)
    o_ref[...] = y
    xn_ref[...] = _rmsnorm_rows(y, gain_ref[...]).astype(xn_ref.dtype)


def _proj_res(a, w, layer, res, gain, *, tm):
    m, k = a.shape
    n = w.shape[2]
    assert m % tm == 0
    return pl.pallas_call(
        _proj_res_kernel,
        out_shape=(jax.ShapeDtypeStruct((m, n), f32), jax.ShapeDtypeStruct((m, n), bf16)),
        grid=(m // tm,),
        in_specs=[
            pl.BlockSpec((tm, k), lambda i: (i, 0)),
            pl.BlockSpec((None, k, n), lambda i: (layer, 0, 0)),
            pl.BlockSpec((tm, n), lambda i: (i, 0)),
            pl.BlockSpec((1, n), lambda i: (0, 0)),
        ],
        out_specs=(pl.BlockSpec((tm, n), lambda i: (i, 0)),
                   pl.BlockSpec((tm, n), lambda i: (i, 0))),
        compiler_params=pltpu.CompilerParams(
            dimension_semantics=("parallel",),
            vmem_limit_bytes=VMEM_LIMIT_BYTES),
        name="proj_res",
    )(a, w, res, gain)


def _mlp_kernel(xn_ref, x_ref, win_ref, wout_ref, fgain_ref, o_ref, *, final_norm):
    f = pl.program_id(1)
    res_rows = x_ref.shape[0]

    @pl.when(f == 0)
    def _():
        def clear(c, _):
            r = pl.multiple_of(c * NORM_ROWS, NORM_ROWS)
            o_ref[pl.ds(r, NORM_ROWS), :] = jnp.zeros((NORM_ROWS, o_ref.shape[1]), f32)
            return 0
        lax.fori_loop(0, o_ref.shape[0] // NORM_ROWS, clear, 0)

    h = jnp.dot(xn_ref[...], win_ref[...], preferred_element_type=f32)
    h = jnp.maximum(h, 0.0)
    h = (h * h).astype(bf16)
    o_ref[...] += jnp.dot(h, wout_ref[...], preferred_element_type=f32)
    r = pl.multiple_of(f * res_rows, res_rows)
    o_ref[pl.ds(r, res_rows), :] += x_ref[...]

    if final_norm:
        @pl.when(f == pl.num_programs(1) - 1)
        def _():
            _store_normed(o_ref, fgain_ref, o_ref)


def _mlp(xn, x, w_in, w_out, layer, final_gain, *, tm, tf, final_norm):
    m, d = x.shape
    d_ff = w_in.shape[2]
    n_f = d_ff // tf
    assert m % tm == 0 and d_ff % tf == 0 and tm % n_f == 0
    return pl.pallas_call(
        functools.partial(_mlp_kernel, final_norm=final_norm),
        out_shape=jax.ShapeDtypeStruct((m, d), f32),
        grid=(m // tm, n_f),
        in_specs=[
            pl.BlockSpec((tm, d), lambda i, f: (i, 0)),
            pl.BlockSpec((tm // n_f, d), lambda i, f: (i * n_f + f, 0)),
            pl.BlockSpec((None, d, tf), lambda i, f: (layer, 0, f)),
            pl.BlockSpec((None, tf, d), lambda i, f: (layer, f, 0)),
            pl.BlockSpec((1, d), lambda i, f: (0, 0)),
        ],
        out_specs=pl.BlockSpec((tm, d), lambda i, f: (i, 0)),
        compiler_params=pltpu.CompilerParams(
            dimension_semantics=("parallel", "arbitrary"),
            vmem_limit_bytes=VMEM_LIMIT_BYTES),
        name="mlp_final" if final_norm else "mlp",
    )(xn, x, w_in, w_out, final_gain)


def _split_head_pair(qp):
    low = lax.broadcasted_iota(jnp.int32, qp.shape, 1) < HEAD_DIM
    zero = jnp.zeros_like(qp)
    return jnp.concatenate([jnp.where(low, qp, zero), jnp.where(low, zero, qp)], axis=0)


def _swa_kernel(sink_ref, q_ref, kp_ref, kc_ref, vp_ref, vc_ref, o_ref, bias_ref, *,
                n_kv, slopes):
    nblk = pl.program_id(1)
    n_heads = n_kv * GROUP_A

    @pl.when((pl.program_id(0) == 0) & (nblk == 0))
    def _():
        kj = lax.broadcasted_iota(jnp.int32, (2 * BLOCK, BLOCK), 0)
        qi = lax.broadcasted_iota(jnp.int32, (2 * BLOCK, BLOCK), 1)
        dist = qi + BLOCK - kj
        in_band = (dist >= 0) & (dist < WINDOW)
        distf = dist.astype(f32)
        for hd in range(n_heads):
            bias = jnp.where(in_band, (-slopes[hd] * LOG2E) * distf, MASK_VALUE)
            bias_ref[0, hd] = bias
            bias_ref[1, hd] = jnp.where(kj >= BLOCK, bias, MASK_VALUE)

    first = (nblk == 0).astype(jnp.int32)
    low_kv = lax.broadcasted_iota(jnp.int32, (2 * BLOCK, LANES), 1) < HEAD_DIM
    pairs = GROUP_A // HEADS_PER_LANE_BLOCK

    def band(prev_ref, cur_ref, h):
        c = h // HEADS_PER_LANE_BLOCK
        cols = slice(c * LANES, (c + 1) * LANES)
        return jnp.concatenate([prev_ref[0, :, cols], cur_ref[0, :, cols]], axis=0)

    def scores(h):
        x = band(kp_ref, kc_ref, h)
        swapped = pltpu.roll(x, HEAD_DIM, axis=1)
        if h % HEADS_PER_LANE_BLOCK == 0:
            k2 = jnp.where(low_kv, x, swapped)
        else:
            k2 = jnp.where(low_kv, swapped, x)
        qs = []
        for p in range(pairs):
            c = h * pairs + p
            qs.append(_split_head_pair(q_ref[0, :, c * LANES:(c + 1) * LANES]))
        return lax.dot_general(k2, jnp.concatenate(qs, axis=0), NT_DIMS,
                               preferred_element_type=f32)

    def softmax_pv(h, s_all):
        lo = (h % HEADS_PER_LANE_BLOCK) * HEAD_DIM
        v_t = band(vp_ref, vc_ref, h).astype(f32).T[lo:lo + HEAD_DIM].astype(bf16)
        ps, inv_denoms = [], []
        for g in range(GROUP_A):
            hd = h * GROUP_A + g
            s = s_all[:, g * BLOCK:(g + 1) * BLOCK] + bias_ref[first, hd]
            sink = sink_ref[hd] * LOG2E
            m = jnp.maximum(jnp.max(s, axis=0, keepdims=True), sink)
            p = jnp.exp2(s - m)
            inv_denoms.append(1.0 / (jnp.sum(p, axis=0, keepdims=True) + jnp.exp2(sink - m)))
            ps.append(p.astype(bf16))
        o_t = jnp.dot(v_t, jnp.concatenate(ps, axis=1), preferred_element_type=f32)
        for p in range(pairs):
            g = HEADS_PER_LANE_BLOCK * p
            pair_t = jnp.concatenate(
                [o_t[:, (g + i) * BLOCK:(g + i + 1) * BLOCK] * inv_denoms[g + i]
                 for i in range(HEADS_PER_LANE_BLOCK)], axis=0)
            c = h * pairs + p
            o_ref[0, :, c * LANES:(c + 1) * LANES] = pair_t.T.astype(o_ref.dtype)

    s_next = scores(0)
    for h in range(n_kv):
        s_cur = s_next
        if h + 1 < n_kv:
            s_next = scores(h + 1)
        softmax_pv(h, s_cur)


def _swa_attention(qkv, sinks, *, n_heads, n_kv):
    b, s, _ = qkv.shape
    q_w = n_heads * HEAD_DIM
    kv_w = n_kv * HEAD_DIM
    assert q_w % kv_w == 0 and kv_w % LANES == 0
    k_col = q_w // kv_w
    v_col = k_col + 1
    slopes = tuple(2.0 ** (-8.0 * (h + 1.0) / n_heads) for h in range(n_heads))
    prev = lambda n: jnp.maximum(n - 1, 0)
    return pl.pallas_call(
        functools.partial(_swa_kernel, n_kv=n_kv, slopes=slopes),
        out_shape=jax.ShapeDtypeStruct((b, s, q_w), bf16),
        grid=(b, s // BLOCK),
        in_specs=[
            pl.BlockSpec(memory_space=pltpu.SMEM),
            pl.BlockSpec((1, BLOCK, q_w), lambda bi, n: (bi, n, 0)),
            pl.BlockSpec((1, BLOCK, kv_w), lambda bi, n: (bi, prev(n), k_col)),
            pl.BlockSpec((1, BLOCK, kv_w), lambda bi, n: (bi, n, k_col)),
            pl.BlockSpec((1, BLOCK, kv_w), lambda bi, n: (bi, prev(n), v_col)),
            pl.BlockSpec((1, BLOCK, kv_w), lambda bi, n: (bi, n, v_col)),
        ],
        out_specs=pl.BlockSpec((1, BLOCK, q_w), lambda bi, n: (bi, n, 0)),
        scratch_shapes=[pltpu.VMEM((2, n_heads, 2 * BLOCK, BLOCK), f32)],
        compiler_params=pltpu.CompilerParams(
            dimension_semantics=("arbitrary", "arbitrary"),
            vmem_limit_bytes=VMEM_LIMIT_BYTES),
        name="swa_attention",
    )(sinks, qkv, qkv, qkv, qkv, qkv)


def _sb_kernel(q_ref, k_ref, v_ref, o_ref, qm_ref, carry_ref, acc_ref, cmax_ref):
    qblk = pl.program_id(2)
    n_pairs = qm_ref.shape[0]
    rows = HEADS_PER_LANE_BLOCK * BLOCK
    row = lax.broadcasted_iota(jnp.int32, (rows, BLOCK), 0) % BLOCK
    col = lax.broadcasted_iota(jnp.int32, (rows, BLOCK), 1)
    before = col < row
    jj = lax.broadcasted_iota(jnp.int32, (BLOCK, BLOCK), 0)
    ss = lax.broadcasted_iota(jnp.int32, (BLOCK, BLOCK), 1)
    tri = jnp.where(jj > ss, 1.0, 0.0).astype(bf16)

    for c in range(n_pairs):
        qm_ref[c] = _split_head_pair(q_ref[0, :, c * LANES:(c + 1) * LANES])

    def top_rows(x):
        return jnp.concatenate([x[:SB_TOP_ROWS], x[BLOCK:BLOCK + SB_TOP_ROWS]], axis=0)

    def other_rows(x):
        return jnp.concatenate([x[SB_TOP_ROWS:BLOCK], x[BLOCK + SB_TOP_ROWS:]], axis=0)

    def sweep(j, *, n_blocks, diag, top_only=False):
        keys = n_blocks * BLOCK
        r = pl.multiple_of((j - (n_blocks - 1)) * BLOCK, BLOCK)
        rows = HEADS_PER_LANE_BLOCK * (SB_TOP_ROWS if top_only else BLOCK)
        ys = []
        for c in range(n_pairs):
            k = k_ref[0, pl.ds(r, keys), c * LANES:(c + 1) * LANES]
            q = top_rows(qm_ref[c]) if top_only else qm_ref[c]
            ys.append(lax.dot_general(q, k, NT_DIMS, preferred_element_type=f32))
        log_betas, parts, totals = [], [], []
        for y in ys:
            yc = jnp.minimum(y, EXP2_CLAMP)
            log_1m = jnp.log(1.0 + jnp.exp2(yc)) * (-LOG2E)
            log_betas.append(log_1m + yc)
            for kb in range(n_blocks):
                part = log_1m[:, kb * BLOCK:(kb + 1) * BLOCK]
                if diag and kb == n_blocks - 1:
                    part = jnp.where(before, part, 0.0)
                totals.append(jnp.sum(part, axis=-1, keepdims=True))
                parts.append(part.astype(bf16))
        suffix = jnp.dot(jnp.concatenate(parts, axis=0), tri, preferred_element_type=f32)
        cmax = None
        for c in range(n_pairs):
            if diag:
                run = None
            else:
                run = top_rows(carry_ref[c]) if top_only else carry_ref[c]
            weights = [None] * n_blocks
            for kb in reversed(range(n_blocks)):
                i_kb = c * n_blocks + kb
                x = (log_betas[c][:, kb * BLOCK:(kb + 1) * BLOCK]
                     + suffix[i_kb * rows:(i_kb + 1) * rows])
                if run is not None:
                    x = x + run
                a = jnp.exp2(x)
                if diag and kb == n_blocks - 1:
                    a = jnp.where(before, a, 0.0)
                weights[kb] = a.astype(bf16)
                run = totals[i_kb] if run is None else run + totals[i_kb]
            a_all = weights[0] if n_blocks == 1 else jnp.concatenate(weights, axis=1)
            v = v_ref[0, pl.ds(r, keys), c * LANES:(c + 1) * LANES]
            pv = jnp.dot(a_all, v, preferred_element_type=f32)
            carry = jnp.broadcast_to(run, (rows, LANES))
            if diag:
                acc_ref[c] = pv
                carry_ref[c] = carry
            elif top_only:
                for half in range(HEADS_PER_LANE_BLOCK):
                    dst = pl.ds(half * BLOCK, SB_TOP_ROWS)
                    src = slice(half * SB_TOP_ROWS, (half + 1) * SB_TOP_ROWS)
                    acc_ref[c, dst, :] += pv[src]
                    carry_ref[c, dst, :] = carry[src]
            else:
                acc_ref[c] += pv
                carry_ref[c] = carry
            cmax = run if cmax is None else jnp.maximum(cmax, run)
        if top_only:
            return jnp.max(cmax), None
        return jnp.max(top_rows(cmax)), jnp.max(other_rows(cmax))

    for n_blocks in range(1, SB_FIRST_BLOCKS + 1):
        taken = (qblk == n_blocks - 1) if n_blocks < SB_FIRST_BLOCKS else (qblk >= n_blocks - 1)

        @pl.when(taken)
        def _(n_blocks=n_blocks):
            cmax_ref[0], cmax_ref[1] = sweep(qblk, n_blocks=n_blocks, diag=True)

    def cond(st):
        j, cmax_top, cmax_other = st
        return (j >= 0) & (jnp.maximum(cmax_top, cmax_other) > LOG2_F32_UNDERFLOW)

    def body(st):
        j, _, cmax_other = st

        @pl.when(cmax_other > LOG2_F32_UNDERFLOW)
        def _():
            cmax_ref[0], cmax_ref[1] = sweep(j, n_blocks=1, diag=False)

        @pl.when(cmax_other <= LOG2_F32_UNDERFLOW)
        def _():
            cmax_ref[0], _ = sweep(j, n_blocks=1, diag=False, top_only=True)

        return j - 1, cmax_ref[0], cmax_ref[1]

    lax.while_loop(cond, body, (qblk - SB_FIRST_BLOCKS, cmax_ref[0], cmax_ref[1]))

    low = lax.broadcasted_iota(jnp.int32, (BLOCK, LANES), 1) < HEAD_DIM
    for c in range(n_pairs):
        o_ref[0, :, c * LANES:(c + 1) * LANES] = jnp.where(
            low, acc_ref[c, :BLOCK], acc_ref[c, BLOCK:]).astype(o_ref.dtype)


def _sb_attention(qkv, *, n_heads):
    b, s, _ = qkv.shape
    w = n_heads * HEAD_DIM
    n_pairs = SB_HEADS // HEADS_PER_LANE_BLOCK
    width = SB_HEADS * HEAD_DIM
    assert w % width == 0
    n_col = w // width
    rows = HEADS_PER_LANE_BLOCK * BLOCK
    return pl.pallas_call(
        _sb_kernel,
        out_shape=jax.ShapeDtypeStruct((b, s, w), bf16),
        grid=(b, n_col, s // BLOCK),
        in_specs=[
            pl.BlockSpec((1, BLOCK, width), lambda bi, c, n: (bi, n, c)),
            pl.BlockSpec((1, s, width), lambda bi, c, n: (bi, 0, n_col + c)),
            pl.BlockSpec((1, s, width), lambda bi, c, n: (bi, 0, 2 * n_col + c)),
        ],
        out_specs=pl.BlockSpec((1, BLOCK, width), lambda bi, c, n: (bi, n, c)),
        scratch_shapes=[
            pltpu.VMEM((n_pairs, rows, LANES), bf16),
            pltpu.VMEM((n_pairs, rows, LANES), f32),
            pltpu.VMEM((n_pairs, rows, LANES), f32),
            pltpu.SMEM((2,), f32),
        ],
        compiler_params=pltpu.CompilerParams(
            dimension_semantics=("parallel", "parallel", "arbitrary"),
            vmem_limit_bytes=VMEM_LIMIT_BYTES),
        name="sb_attention",
    )(qkv, qkv, qkv)


def kernel(x, a_w_qkv, a_w_o, a_sinks, b_w_qkv, b_w_o, norm_mix, norm_mlp,
           mlp_w_in, mlp_w_out, final_norm):
    b, s, d = x.shape
    depth = norm_mix.shape[0]
    n_heads_a = a_w_o.shape[1] // HEAD_DIM
    n_kv_a = (a_w_qkv.shape[2] // HEAD_DIM - n_heads_a) // 2
    n_heads_b = b_w_o.shape[1] // HEAD_DIM
    assert n_heads_a == n_kv_a * GROUP_A and s % BLOCK == 0

    def folded(w_qkv, q_width, gains):
        col = lax.broadcasted_iota(jnp.int32, (1, 1, w_qkv.shape[2]), 2)
        col_scale = jnp.where(col < q_width, SCALE * LOG2E, 1.0).astype(f32)
        return w_qkv * col_scale * gains[:, :, None]

    a_w_qkv = folded(a_w_qkv, n_heads_a * HEAD_DIM, norm_mix[0::N_MIXERS])
    b_w_qkv = folded(b_w_qkv, n_heads_b * HEAD_DIM, norm_mix[1::N_MIXERS])
    a_w_qkv, a_w_o, b_w_qkv, b_w_o, mlp_w_in, mlp_w_out = (
        w.astype(bf16) for w in (a_w_qkv, a_w_o, b_w_qkv, b_w_o, mlp_w_in, mlp_w_out))
    xf = x.reshape(b * s, d)
    final_gain = final_norm.reshape(1, d)
    for i in range(depth):
        j = i // N_MIXERS
        if i % N_MIXERS == 0:
            qkv = _norm_proj(xf, a_w_qkv, j, tm=1024, tn=1280)
            att = _swa_attention(qkv.reshape(b, s, -1), a_sinks[j],
                                 n_heads=n_heads_a, n_kv=n_kv_a)
            w_o = a_w_o
        else:
            qkv = _norm_proj(xf, b_w_qkv, j, tm=1024, tn=1024)
            att = _sb_attention(qkv.reshape(b, s, -1), n_heads=n_heads_b)
            w_o = b_w_o
        xf, xn = _proj_res(att.reshape(b * s, -1), w_o, j, xf, norm_mlp[i].reshape(1, d), tm=512)
        xf = _mlp(xn, xf, mlp_w_in, mlp_w_out, i, final_gain,
                  tm=1024, tf=1024, final_norm=(i == depth - 1))
    return xf.reshape(b, s, d)
```

```python
import functools
import math

import jax
import jax.numpy as jnp
from jax import lax
from jax.experimental import pallas as pl
from jax.experimental.pallas import tpu as pltpu

HEAD_DIM = 64
GROUP_A = 8
WINDOW = 128
BLOCK = 128
RMS_EPS = 1e-5
N_MIXERS = 2
LANES = 128
HEADS_PER_LANE_BLOCK = LANES // HEAD_DIM
MASK_VALUE = -1e30
LOG2E = math.log2(math.e)
LOG2_F32_UNDERFLOW = -150.0
SCALE = 1.0 / math.sqrt(HEAD_DIM)

VMEM_LIMIT_BYTES = 56 * 1024 * 1024
NORM_ROWS = 256
SB_HEADS = 16
SB_FIRST_BLOCKS = 3
EXP2_CLAMP = 126.0
SB_TOP_ROWS = 64
SB_FIRST_LAGS = (1, 2)

bf16 = jnp.bfloat16
f32 = jnp.float32
NT_DIMS = (((1,), (1,)), ((), ()))


def _rmsnorm_rows(x, gain):
    y = x * lax.rsqrt(jnp.mean(x * x, axis=-1, keepdims=True) + RMS_EPS)
    return y * gain


def _store_normed(x_ref, gain_ref, xn_ref):
    rows = x_ref.shape[0]

    def step(c, _):
        r = pl.multiple_of(c * NORM_ROWS, NORM_ROWS)
        x = x_ref[pl.ds(r, NORM_ROWS), :]
        xn_ref[pl.ds(r, NORM_ROWS), :] = _rmsnorm_rows(x, gain_ref[...]).astype(xn_ref.dtype)
        return 0

    lax.fori_loop(0, rows // NORM_ROWS, step, 0)


def _norm_proj_kernel(x_ref, w_ref, o_ref, xb_ref, rinv_ref):
    @pl.when(pl.program_id(1) == 0)
    def _():
        def step(c, _):
            rows = pl.ds(pl.multiple_of(c * NORM_ROWS, NORM_ROWS), NORM_ROWS)
            x = x_ref[rows, :]
            xb_ref[rows, :] = x.astype(xb_ref.dtype)
            rinv_ref[rows, :] = lax.rsqrt(jnp.mean(x * x, axis=-1, keepdims=True) + RMS_EPS)
            return 0
        lax.fori_loop(0, x_ref.shape[0] // NORM_ROWS, step, 0)

    y = jnp.dot(xb_ref[...], w_ref[...], preferred_element_type=f32)
    o_ref[...] = (y * rinv_ref[...]).astype(o_ref.dtype)


def _norm_proj(x, w, layer, *, tm, tn):
    m, d = x.shape
    n = w.shape[2]
    assert m % tm == 0 and n % tn == 0
    return pl.pallas_call(
        _norm_proj_kernel,
        out_shape=jax.ShapeDtypeStruct((m, n), bf16),
        grid=(m // tm, n // tn),
        in_specs=[
            pl.BlockSpec((tm, d), lambda i, j: (i, 0)),
            pl.BlockSpec((None, d, tn), lambda i, j: (layer, 0, j)),
        ],
        out_specs=pl.BlockSpec((tm, tn), lambda i, j: (i, j)),
        scratch_shapes=[pltpu.VMEM((tm, d), bf16), pltpu.VMEM((tm, 1), f32)],
        compiler_params=pltpu.CompilerParams(
            dimension_semantics=("parallel", "arbitrary"),
            vmem_limit_bytes=VMEM_LIMIT_BYTES),
        name="norm_proj",
    )(x, w)


def _proj_res_kernel(a_ref, w_ref, r_ref, gain_ref, o_ref, xn_ref):
    y = r_ref[...] + jnp.dot(a_ref[...], w_ref[...], preferred_element_type=f32)
    o_ref[...] = y
    xn_ref[...] = _rmsnorm_rows(y, gain_ref[...]).astype(xn_ref.dtype)


def _proj_res(a, w, layer, res, gain, *, tm):
    m, k = a.shape
    n = w.shape[2]
    assert m % tm == 0
    return pl.pallas_call(
        _proj_res_kernel,
        out_shape=(jax.ShapeDtypeStruct((m, n), f32), jax.ShapeDtypeStruct((m, n), bf16)),
        grid=(m // tm,),
        in_specs=[
            pl.BlockSpec((tm, k), lambda i: (i, 0)),
            pl.BlockSpec((None, k, n), lambda i: (layer, 0, 0)),
            pl.BlockSpec((tm, n), lambda i: (i, 0)),
            pl.BlockSpec((1, n), lambda i: (0, 0)),
        ],
        out_specs=(pl.BlockSpec((tm, n), lambda i: (i, 0)),
                   pl.BlockSpec((tm, n), lambda i: (i, 0))),
        compiler_params=pltpu.CompilerParams(
            dimension_semantics=("parallel",),
            vmem_limit_bytes=VMEM_LIMIT_BYTES),
        name="proj_res",
    )(a, w, res, gain)


def _mlp_kernel(xn_ref, x_ref, win_ref, wout_ref, fgain_ref, o_ref, *, final_norm):
    f = pl.program_id(1)
    res_rows = x_ref.shape[0]

    @pl.when(f == 0)
    def _():
        def clear(c, _):
            r = pl.multiple_of(c * NORM_ROWS, NORM_ROWS)
            o_ref[pl.ds(r, NORM_ROWS), :] = jnp.zeros((NORM_ROWS, o_ref.shape[1]), f32)
            return 0
        lax.fori_loop(0, o_ref.shape[0] // NORM_ROWS, clear, 0)

    h = jnp.dot(xn_ref[...], win_ref[...], preferred_element_type=f32)
    h = jnp.maximum(h, 0.0)
    h = (h * h).astype(bf16)
    o_ref[...] += jnp.dot(h, wout_ref[...], preferred_element_type=f32)
    r = pl.multiple_of(f * res_rows, res_rows)
    o_ref[pl.ds(r, res_rows), :] += x_ref[...]

    if final_norm:
        @pl.when(f == pl.num_programs(1) - 1)
        def _():
            _store_normed(o_ref, fgain_ref, o_ref)


def _mlp(xn, x, w_in, w_out, layer, final_gain, *, tm, tf, final_norm):
    m, d = x.shape
    d_ff = w_in.shape[2]
    n_f = d_ff // tf
    assert m % tm == 0 and d_ff % tf == 0 and tm % n_f == 0
    return pl.pallas_call(
        functools.partial(_mlp_kernel, final_norm=final_norm),
        out_shape=jax.ShapeDtypeStruct((m, d), f32),
        grid=(m // tm, n_f),
        in_specs=[
            pl.BlockSpec((tm, d), lambda i, f: (i, 0)),
            pl.BlockSpec((tm // n_f, d), lambda i, f: (i * n_f + f, 0)),
            pl.BlockSpec((None, d, tf), lambda i, f: (layer, 0, f)),
            pl.BlockSpec((None, tf, d), lambda i, f: (layer, f, 0)),
            pl.BlockSpec((1, d), lambda i, f: (0, 0)),
        ],
        out_specs=pl.BlockSpec((tm, d), lambda i, f: (i, 0)),
        compiler_params=pltpu.CompilerParams(
            dimension_semantics=("parallel", "arbitrary"),
            vmem_limit_bytes=VMEM_LIMIT_BYTES),
        name="mlp_final" if final_norm else "mlp",
    )(xn, x, w_in, w_out, final_gain)


def _split_head_pair(qp):
    low = lax.broadcasted_iota(jnp.int32, qp.shape, 1) < HEAD_DIM
    zero = jnp.zeros_like(qp)
    return jnp.concatenate([jnp.where(low, qp, zero), jnp.where(low, zero, qp)], axis=0)


def _swa_kernel(sink_ref, q_ref, kp_ref, kc_ref, vp_ref, vc_ref, o_ref, bias_ref, *,
                n_kv, slopes):
    nblk = pl.program_id(1)
    n_heads = n_kv * GROUP_A

    @pl.when((pl.program_id(0) == 0) & (nblk == 0))
    def _():
        kj = lax.broadcasted_iota(jnp.int32, (2 * BLOCK, BLOCK), 0)
        qi = lax.broadcasted_iota(jnp.int32, (2 * BLOCK, BLOCK), 1)
        dist = qi + BLOCK - kj
        in_band = (dist >= 0) & (dist < WINDOW)
        distf = dist.astype(f32)
        for hd in range(n_heads):
            bias = jnp.where(in_band, (-slopes[hd] * LOG2E) * distf, MASK_VALUE)
            bias_ref[0, hd] = bias
            bias_ref[1, hd] = jnp.where(kj >= BLOCK, bias, MASK_VALUE)

    first = (nblk == 0).astype(jnp.int32)
    low_kv = lax.broadcasted_iota(jnp.int32, (2 * BLOCK, LANES), 1) < HEAD_DIM
    pairs = GROUP_A // HEADS_PER_LANE_BLOCK

    def band(prev_ref, cur_ref, h):
        c = h // HEADS_PER_LANE_BLOCK
        cols = slice(c * LANES, (c + 1) * LANES)
        return jnp.concatenate([prev_ref[0, :, cols], cur_ref[0, :, cols]], axis=0)

    def scores(h):
        x = band(kp_ref, kc_ref, h)
        swapped = pltpu.roll(x, HEAD_DIM, axis=1)
        if h % HEADS_PER_LANE_BLOCK == 0:
            k2 = jnp.where(low_kv, x, swapped)
        else:
            k2 = jnp.where(low_kv, swapped, x)
        qs = []
        for p in range(pairs):
            c = h * pairs + p
            qs.append(_split_head_pair(q_ref[0, :, c * LANES:(c + 1) * LANES]))
        return lax.dot_general(k2, jnp.concatenate(qs, axis=0), NT_DIMS,
                               preferred_element_type=f32)

    def softmax_pv(h, s_all):
        lo = (h % HEADS_PER_LANE_BLOCK) * HEAD_DIM
        v_t = band(vp_ref, vc_ref, h).astype(f32).T[lo:lo + HEAD_DIM].astype(bf16)
        ps, inv_denoms = [], []
        for g in range(GROUP_A):
            hd = h * GROUP_A + g
            s = s_all[:, g * BLOCK:(g + 1) * BLOCK] + bias_ref[first, hd]
            sink = sink_ref[hd] * LOG2E
            m = jnp.maximum(jnp.max(s, axis=0, keepdims=True), sink)
            p = jnp.exp2(s - m)
            inv_denoms.append(1.0 / (jnp.sum(p, axis=0, keepdims=True) + jnp.exp2(sink - m)))
            ps.append(p.astype(bf16))
        o_t = jnp.dot(v_t, jnp.concatenate(ps, axis=1), preferred_element_type=f32)
        for p in range(pairs):
            g = HEADS_PER_LANE_BLOCK * p
            pair_t = jnp.concatenate(
                [o_t[:, (g + i) * BLOCK:(g + i + 1) * BLOCK] * inv_denoms[g + i]
                 for i in range(HEADS_PER_LANE_BLOCK)], axis=0)
            c = h * pairs + p
            o_ref[0, :, c * LANES:(c + 1) * LANES] = pair_t.T.astype(o_ref.dtype)

    s_next = scores(0)
    for h in range(n_kv):
        s_cur = s_next
        if h + 1 < n_kv:
            s_next = scores(h + 1)
        softmax_pv(h, s_cur)


def _swa_attention(qkv, sinks, *, n_heads, n_kv):
    b, s, _ = qkv.shape
    q_w = n_heads * HEAD_DIM
    kv_w = n_kv * HEAD_DIM
    assert q_w % kv_w == 0 and kv_w % LANES == 0
    k_col = q_w // kv_w
    v_col = k_col + 1
    slopes = tuple(2.0 ** (-8.0 * (h + 1.0) / n_heads) for h in range(n_heads))
    prev = lambda n: jnp.maximum(n - 1, 0)
    return pl.pallas_call(
        functools.partial(_swa_kernel, n_kv=n_kv, slopes=slopes),
        out_shape=jax.ShapeDtypeStruct((b, s, q_w), bf16),
        grid=(b, s // BLOCK),
        in_specs=[
            pl.BlockSpec(memory_space=pltpu.SMEM),
            pl.BlockSpec((1, BLOCK, q_w), lambda bi, n: (bi, n, 0)),
            pl.BlockSpec((1, BLOCK, kv_w), lambda bi, n: (bi, prev(n), k_col)),
            pl.BlockSpec((1, BLOCK, kv_w), lambda bi, n: (bi, n, k_col)),
            pl.BlockSpec((1, BLOCK, kv_w), lambda bi, n: (bi, prev(n), v_col)),
            pl.BlockSpec((1, BLOCK, kv_w), lambda bi, n: (bi, n, v_col)),
        ],
        out_specs=pl.BlockSpec((1, BLOCK, q_w), lambda bi, n: (bi, n, 0)),
        scratch_shapes=[pltpu.VMEM((2, n_heads, 2 * BLOCK, BLOCK), f32)],
        compiler_params=pltpu.CompilerParams(
            dimension_semantics=("arbitrary", "arbitrary"),
            vmem_limit_bytes=VMEM_LIMIT_BYTES),
        name="swa_attention",
    )(sinks, qkv, qkv, qkv, qkv, qkv)


def _sb_kernel(q_ref, k_ref, v_ref, o_ref, qm_ref, carry_ref, acc_ref, cmax_ref):
    qblk = pl.program_id(2)
    n_pairs = qm_ref.shape[0]
    rows = HEADS_PER_LANE_BLOCK * BLOCK
    row = lax.broadcasted_iota(jnp.int32, (rows, BLOCK), 0) % BLOCK
    col = lax.broadcasted_iota(jnp.int32, (rows, BLOCK), 1)
    before = col < row
    jj = lax.broadcasted_iota(jnp.int32, (BLOCK, BLOCK), 0)
    ss = lax.broadcasted_iota(jnp.int32, (BLOCK, BLOCK), 1)
    tri = jnp.where(jj > ss, 1.0, 0.0).astype(bf16)

    for c in range(n_pairs):
        qm_ref[c] = _split_head_pair(q_ref[0, :, c * LANES:(c + 1) * LANES])

    def top_rows(x):
        return jnp.concatenate([x[:SB_TOP_ROWS], x[BLOCK:BLOCK + SB_TOP_ROWS]], axis=0)

    def other_rows(x):
        return jnp.concatenate([x[SB_TOP_ROWS:BLOCK], x[BLOCK + SB_TOP_ROWS:]], axis=0)

    def sweep(j, *, n_blocks, diag, top_only=False, lags=None):
        keys = n_blocks * BLOCK
        r = pl.multiple_of((j - (n_blocks - 1)) * BLOCK, BLOCK)
        rows = HEADS_PER_LANE_BLOCK * (SB_TOP_ROWS if top_only else BLOCK)

        def score_stage(c):
            k = k_ref[0, pl.ds(r, keys), c * LANES:(c + 1) * LANES]
            q = top_rows(qm_ref[c]) if top_only else qm_ref[c]
            return lax.dot_general(q, k, NT_DIMS, preferred_element_type=f32)

        def log_stage(y):
            yc = jnp.minimum(y, EXP2_CLAMP)
            log_1m = jnp.log(1.0 + jnp.exp2(yc)) * (-LOG2E)
            parts, totals = [], []
            for kb in range(n_blocks):
                part = log_1m[:, kb * BLOCK:(kb + 1) * BLOCK]
                if diag and kb == n_blocks - 1:
                    part = jnp.where(before, part, 0.0)
                totals.append(jnp.sum(part, axis=-1, keepdims=True))
                parts.append(part.astype(bf16))
            return log_1m + yc, parts, totals

        def suffix_stage(parts):
            return jnp.dot(jnp.concatenate(parts, axis=0), tri, preferred_element_type=f32)

        def weight_stage(c, log_beta, suffix, totals):
            if diag:
                run = None
            else:
                run = top_rows(carry_ref[c]) if top_only else carry_ref[c]
            weights = [None] * n_blocks
            for kb in reversed(range(n_blocks)):
                x = log_beta[:, kb * BLOCK:(kb + 1) * BLOCK] + suffix[kb * rows:(kb + 1) * rows]
                if run is not None:
                    x = x + run
                a = jnp.exp2(x)
                if diag and kb == n_blocks - 1:
                    a = jnp.where(before, a, 0.0)
                weights[kb] = a.astype(bf16)
                run = totals[kb] if run is None else run + totals[kb]
            a_all = weights[0] if n_blocks == 1 else jnp.concatenate(weights, axis=1)
            v = v_ref[0, pl.ds(r, keys), c * LANES:(c + 1) * LANES]
            pv = jnp.dot(a_all, v, preferred_element_type=f32)
            carry = jnp.broadcast_to(run, (rows, LANES))
            if diag:
                acc_ref[c] = pv
                carry_ref[c] = carry
            elif top_only:
                for half in range(HEADS_PER_LANE_BLOCK):
                    dst = pl.ds(half * BLOCK, SB_TOP_ROWS)
                    src = slice(half * SB_TOP_ROWS, (half + 1) * SB_TOP_ROWS)
                    acc_ref[c, dst, :] += pv[src]
                    carry_ref[c, dst, :] = carry[src]
            else:
                acc_ref[c] += pv
                carry_ref[c] = carry
            return run

        runs = []
        if lags is None:
            logs = [log_stage(score_stage(c)) for c in range(n_pairs)]
            suffix = suffix_stage([p for _, parts, _ in logs for p in parts])
            for c, (log_beta, _, totals) in enumerate(logs):
                runs.append(weight_stage(
                    c, log_beta, suffix[c * n_blocks * rows:(c + 1) * n_blocks * rows], totals))
        else:
            ys, mids = {}, {}
            for t in range(n_pairs + lags[1]):
                if t < n_pairs:
                    ys[t] = score_stage(t)
                if 0 <= t - lags[0] < n_pairs:
                    log_beta, parts, totals = log_stage(ys.pop(t - lags[0]))
                    mids[t - lags[0]] = (log_beta, suffix_stage(parts), totals)
                if 0 <= t - lags[1] < n_pairs:
                    runs.append(weight_stage(t - lags[1], *mids.pop(t - lags[1])))
        cmax = functools.reduce(jnp.maximum, runs)
        if top_only:
            return jnp.max(cmax), None
        return jnp.max(top_rows(cmax)), jnp.max(other_rows(cmax))

    for n_blocks in range(1, SB_FIRST_BLOCKS + 1):
        taken = (qblk == n_blocks - 1) if n_blocks < SB_FIRST_BLOCKS else (qblk >= n_blocks - 1)

        @pl.when(taken)
        def _(n_blocks=n_blocks):
            cmax_ref[0], cmax_ref[1] = sweep(qblk, n_blocks=n_blocks, diag=True,
                                             lags=SB_FIRST_LAGS)

    def cond(st):
        j, cmax_top, cmax_other = st
        return (j >= 0) & (jnp.maximum(cmax_top, cmax_other) > LOG2_F32_UNDERFLOW)

    def body(st):
        j, _, cmax_other = st

        @pl.when(cmax_other > LOG2_F32_UNDERFLOW)
        def _():
            cmax_ref[0], cmax_ref[1] = sweep(j, n_blocks=1, diag=False)

        @pl.when(cmax_other <= LOG2_F32_UNDERFLOW)
        def _():
            cmax_ref[0], _ = sweep(j, n_blocks=1, diag=False, top_only=True)

        return j - 1, cmax_ref[0], cmax_ref[1]

    lax.while_loop(cond, body, (qblk - SB_FIRST_BLOCKS, cmax_ref[0], cmax_ref[1]))

    low = lax.broadcasted_iota(jnp.int32, (BLOCK, LANES), 1) < HEAD_DIM
    for c in range(n_pairs):
        o_ref[0, :, c * LANES:(c + 1) * LANES] = jnp.where(
            low, acc_ref[c, :BLOCK], acc_ref[c, BLOCK:]).astype(o_ref.dtype)


def _sb_attention(qkv, *, n_heads):
    b, s, _ = qkv.shape
    w = n_heads * HEAD_DIM
    n_pairs = SB_HEADS // HEADS_PER_LANE_BLOCK
    width = SB_HEADS * HEAD_DIM
    assert w % width == 0
    n_col = w // width
    rows = HEADS_PER_LANE_BLOCK * BLOCK
    return pl.pallas_call(
        _sb_kernel,
        out_shape=jax.ShapeDtypeStruct((b, s, w), bf16),
        grid=(b, n_col, s // BLOCK),
        in_specs=[
            pl.BlockSpec((1, BLOCK, width), lambda bi, c, n: (bi, n, c)),
            pl.BlockSpec((1, s, width), lambda bi, c, n: (bi, 0, n_col + c)),
            pl.BlockSpec((1, s, width), lambda bi, c, n: (bi, 0, 2 * n_col + c)),
        ],
        out_specs=pl.BlockSpec((1, BLOCK, width), lambda bi, c, n: (bi, n, c)),
        scratch_shapes=[
            pltpu.VMEM((n_pairs, rows, LANES), bf16),
            pltpu.VMEM((n_pairs, rows, LANES), f32),
            pltpu.VMEM((n_pairs, rows, LANES), f32),
            pltpu.SMEM((2,), f32),
        ],
        compiler_params=pltpu.CompilerParams(
            dimension_semantics=("parallel", "parallel", "arbitrary"),
            vmem_limit_bytes=VMEM_LIMIT_BYTES),
        name="sb_attention",
    )(qkv, qkv, qkv)


def kernel(x, a_w_qkv, a_w_o, a_sinks, b_w_qkv, b_w_o, norm_mix, norm_mlp,
           mlp_w_in, mlp_w_out, final_norm):
    b, s, d = x.shape
    depth = norm_mix.shape[0]
    n_heads_a = a_w_o.shape[1] // HEAD_DIM
    n_kv_a = (a_w_qkv.shape[2] // HEAD_DIM - n_heads_a) // 2
    n_heads_b = b_w_o.shape[1] // HEAD_DIM
    assert n_heads_a == n_kv_a * GROUP_A and s % BLOCK == 0

    def folded(w_qkv, q_width, gains):
        col = lax.broadcasted_iota(jnp.int32, (1, 1, w_qkv.shape[2]), 2)
        col_scale = jnp.where(col < q_width, SCALE * LOG2E, 1.0).astype(f32)
        return w_qkv * col_scale * gains[:, :, None]

    a_w_qkv = folded(a_w_qkv, n_heads_a * HEAD_DIM, norm_mix[0::N_MIXERS])
    b_w_qkv = folded(b_w_qkv, n_heads_b * HEAD_DIM, norm_mix[1::N_MIXERS])
    a_w_qkv, a_w_o, b_w_qkv, b_w_o, mlp_w_in, mlp_w_out = (
        w.astype(bf16) for w in (a_w_qkv, a_w_o, b_w_qkv, b_w_o, mlp_w_in, mlp_w_out))
    xf = x.reshape(b * s, d)
    final_gain = final_norm.reshape(1, d)
    for i in range(depth):
        j = i // N_MIXERS
        if i % N_MIXERS == 0:
            qkv = _norm_proj(xf, a_w_qkv, j, tm=1024, tn=1280)
            att = _swa_attention(qkv.reshape(b, s, -1), a_sinks[j],
                                 n_heads=n_heads_a, n_kv=n_kv_a)
            w_o = a_w_o
        else:
            qkv = _norm_proj(xf, b_w_qkv, j, tm=1024, tn=1024)
            att = _sb_attention(qkv.reshape(b, s, -1), n_heads=n_heads_b)
            w_o = b_w_o
        xf, xn = _proj_res(att.reshape(b * s, -1), w_o, j, xf, norm_mlp[i].reshape(1, d), tm=512)
        xf = _mlp(xn, xf, mlp_w_in, mlp_w_out, i, final_gain,
                  tm=1024, tf=1024, final_norm=(i == depth - 1))
    return xf.reshape(b, s, d)
```
